```python
import math
import jax
import jax.numpy as jnp
from jax import lax
import numpy as np

D_MODEL = 1024
BATCH = 16
SEQ = 256
DEPTH = 4
DEC_BATCH = 4
DEC_SEQ = 2048
PAST_LEN = 256

GRID_W = 64
HEAD_DIM = 64
H_A = 6
NA_ROWS = 8
NA_COLS = 16
NA_QCOLS = 16
NA_KCOLS = 32
H_B = 4
DQK_B = 32
DV_B = 64
H_C = 6
KV_C = 2
MIX_WIDTH = H_A * HEAD_DIM + H_B * DV_B + H_C * HEAD_DIM
IN_SIZES = (H_A * HEAD_DIM, H_A * HEAD_DIM, H_A * HEAD_DIM,
            H_B * 2 * DQK_B, H_B * 2 * DQK_B, H_B * DV_B,
            H_C * HEAD_DIM, KV_C * HEAD_DIM, KV_C * HEAD_DIM)
IN_WIDTH = sum(IN_SIZES)
N_KEYS = 128
N_EXPERTS = N_KEYS * N_KEYS
PEER_HEADS = 8
PEER_TOPK = 16
PEER_DQ = 256
PEER_TOK_BLOCK = 128

Q_BLOCK = 128
ROPE_THETA = 10000.0
EPS = 1e-6

kernel_name = "hybrid_dit_prefix_natten_diff_gqa_peer"


def rms_norm(x, g):
    xf = x.astype(jnp.float32)
    y = xf * lax.rsqrt(jnp.mean(xf * xf, axis=-1, keepdims=True) + EPS)
    return (y * g.astype(jnp.float32)).astype(x.dtype)


def _rope_1d(x, pos):
    d = x.shape[-1]
    inv = ROPE_THETA ** (-jnp.arange(d // 2, dtype=jnp.float32) / (d // 2))
    ang = pos.astype(jnp.float32)[:, None] * inv[None, :]
    cos = jnp.cos(ang)[None, :, None, :].astype(x.dtype)
    sin = jnp.sin(ang)[None, :, None, :].astype(x.dtype)
    x1, x2 = x[..., : d // 2], x[..., d // 2:]
    return jnp.concatenate([x1 * cos - x2 * sin, x1 * sin + x2 * cos], axis=-1)


def axial_rope(x):
    t = jnp.arange(x.shape[1])
    d = x.shape[-1]
    return jnp.concatenate([_rope_1d(x[..., : d // 2], t // GRID_W),
                            _rope_1d(x[..., d // 2:], t % GRID_W)], axis=-1)


def mixer_qkv(h, w_in_l, qk_norm_l, positioned):
    B, T, _ = h.shape
    splits = np.cumsum(IN_SIZES)[:-1].tolist()
    aq, ak, av, bq, bk, bv, cq, ck, cv = jnp.split(h @ w_in_l, splits, axis=-1)
    aq = aq.reshape(B, T, H_A, HEAD_DIM)
    ak = ak.reshape(B, T, H_A, HEAD_DIM)
    av = av.reshape(B, T, H_A, HEAD_DIM)
    bq = bq.reshape(B, T, H_B, 2, DQK_B)
    bk = bk.reshape(B, T, H_B, 2, DQK_B)
    bv = bv.reshape(B, T, H_B, DV_B)
    cq = rms_norm(cq.reshape(B, T, H_C, HEAD_DIM), qk_norm_l[0])
    ck = rms_norm(ck.reshape(B, T, KV_C, HEAD_DIM), qk_norm_l[1])
    cv = cv.reshape(B, T, KV_C, HEAD_DIM)
    if positioned:
        bq = axial_rope(bq.reshape(B, T, H_B * 2, DQK_B)).reshape(B, T, H_B, 2, DQK_B)
        bk = axial_rope(bk.reshape(B, T, H_B * 2, DQK_B)).reshape(B, T, H_B, 2, DQK_B)
        cq = axial_rope(cq)
        ck = axial_rope(ck)
    return (aq, ak, av), (bq, bk, bv), (cq, ck, cv)


def gqa_attend(q, k, v):
    B, Tq, H, d = q.shape
    kv = k.shape[2]
    g = H // kv
    nb = Tq // Q_BLOCK
    qb = q.reshape(B, nb, Q_BLOCK, kv, g, d).transpose(1, 0, 2, 3, 4, 5)

    def one(qblk):
        s = jnp.einsum('bqkgd,bskd->bkgqs', qblk, k).astype(jnp.float32) * (d ** -0.5)
        p = jax.nn.softmax(s, axis=-1).astype(v.dtype)
        return jnp.einsum('bkgqs,bskd->bqkgd', p, v)

    o = lax.map(one, qb)
    return o.transpose(1, 0, 2, 3, 4, 5).reshape(B, Tq, H * d)


def diff_attend(q, k, v, lam):
    B, Tq, H, _, d = q.shape
    nb = Tq // Q_BLOCK
    qb = q.reshape(B, nb, Q_BLOCK, H, 2, d).transpose(1, 0, 2, 3, 4, 5)

    def one(qblk):
        s = jnp.einsum('bqhcd,bshcd->bhcqs', qblk, k).astype(jnp.float32) * (d ** -0.5)
        p = jax.nn.softmax(s, axis=-1)
        pd = (p[:, :, 0] - lam * p[:, :, 1]).astype(v.dtype)
        return jnp.einsum('bhqs,bshd->bqhd', pd, v)

    o = lax.map(one, qb)
    return o.transpose(1, 0, 2, 3, 4).reshape(B, Tq, H, v.shape[-1])


def neighbourhood_attend(q, k, v, k_ctx, v_ctx, rpb):
    B, T, H, d = q.shape
    rows = T // GRID_W
    wr = min(NA_ROWS, rows)
    ncb = GRID_W // NA_QCOLS
    qc = np.arange(GRID_W)
    cs = np.clip(qc - NA_COLS // 2, 0, GRID_W - NA_COLS)
    kc0 = np.clip(cs[::NA_QCOLS], 0, GRID_W - NA_KCOLS)
    kcols = kc0[:, None] + np.arange(NA_KCOLS)[None, :]
    qcols = qc.reshape(ncb, NA_QCOLS)
    cs_b = cs.reshape(ncb, NA_QCOLS)
    col_ok = (kcols[:, None, :] >= cs_b[:, :, None]) & (kcols[:, None, :] < cs_b[:, :, None] + NA_COLS)
    mask = np.broadcast_to(col_ok[:, :, None, :], (ncb, NA_QCOLS, wr, NA_KCOLS)).reshape(ncb, NA_QCOLS, wr * NA_KCOLS)
    dc_idx = np.clip(kcols[:, None, :] - qcols[:, :, None], -(NA_COLS - 1), NA_COLS - 1) + NA_COLS - 1
    qg = q.reshape(B, rows, ncb, NA_QCOLS, H, d)
    kg = k.reshape(B, rows, GRID_W, H, d)
    vg = v.reshape(B, rows, GRID_W, H, d)
    scale = d ** -0.5
    nk = wr * NA_KCOLS

    def one_row(r):
        rs = jnp.clip(r - wr // 2, 0, rows - wr)
        k_rows = lax.dynamic_slice_in_dim(kg, rs, wr, axis=1)
        v_rows = lax.dynamic_slice_in_dim(vg, rs, wr, axis=1)
        k_blk = k_rows[:, :, kcols].transpose(0, 2, 1, 3, 4, 5).reshape(B, ncb, nk, H, d)
        v_blk = v_rows[:, :, kcols].transpose(0, 2, 1, 3, 4, 5).reshape(B, ncb, nk, H, d)
        q_row = lax.dynamic_index_in_dim(qg, r, axis=1, keepdims=False)
        dr_idx = rs + jnp.arange(wr) - r + NA_ROWS - 1
        bias = rpb[:, dr_idx][:, :, dc_idx]
        bias = bias.transpose(0, 2, 3, 1, 4).reshape(H, ncb, NA_QCOLS, nk).astype(jnp.float32)
        s_lat = jnp.einsum('bnqhd,bnkhd->bhnqk', q_row, k_blk).astype(jnp.float32) * scale + bias[None]
        s_lat = jnp.where(mask[None, None], s_lat, -jnp.inf)
        s_ctx = jnp.einsum('bnqhd,bshd->bhnqs', q_row, k_ctx).astype(jnp.float32) * scale
        p = jax.nn.softmax(jnp.concatenate([s_lat, s_ctx], axis=-1), axis=-1).astype(v.dtype)
        o = (jnp.einsum('bhnqk,bnkhd->bnqhd', p[..., :nk], v_blk)
             + jnp.einsum('bhnqs,bshd->bnqhd', p[..., nk:], v_ctx))
        return o.reshape(B, GRID_W, H, d)

    o = lax.map(one_row, jnp.arange(rows))
    return o.transpose(1, 0, 2, 3, 4).reshape(B, T, H * d)


def peer_ffn(h, wq, keys, u, v):
    B, T, D = h.shape
    x = h.reshape(B * T, D)
    n = x.shape[0]
    q = (x @ wq).reshape(n, PEER_HEADS, 2, PEER_DQ // 2)
    s = jnp.einsum('nhcd,hckd->nhck', q, keys).astype(jnp.float32)
    s1, i1 = lax.top_k(s[:, :, 0], PEER_TOPK)
    s2, i2 = lax.top_k(s[:, :, 1], PEER_TOPK)
    cand = (s1[..., :, None] + s2[..., None, :]).reshape(n, PEER_HEADS, PEER_TOPK * PEER_TOPK)
    sc, ci = lax.top_k(cand, PEER_TOPK)
    e = (jnp.take_along_axis(i1, ci // PEER_TOPK, axis=-1) * N_KEYS
         + jnp.take_along_axis(i2, ci % PEER_TOPK, axis=-1))
    g = jax.nn.softmax(sc, axis=-1)
    nb = n // PEER_TOK_BLOCK
    hk = PEER_HEADS * PEER_TOPK

    def one(args):
        xb, eb, gb = args
        a = jax.nn.gelu(jnp.einsum('tkd,td->tk', u[eb], xb))
        return jnp.einsum('tk,tkd->td', (a * gb.astype(a.dtype)), v[eb])

    y = lax.map(one, (x.reshape(nb, PEER_TOK_BLOCK, D), e.reshape(nb, PEER_TOK_BLOCK, hk),
                      g.reshape(nb, PEER_TOK_BLOCK, hk)))
    return y.reshape(B, T, D)


def trunk_layer(x, mod, l, ctx, w_in, w_out, norm_g, na_rpb, diff_lambda, diff_subln,
                gqa_qk_norm, peer_wq, peer_keys, peer_u, peer_v):
    B, T, _ = x.shape
    latent = ctx is not None
    shift1, scale1, gate1, shift2, scale2, gate2 = jnp.split(mod, 6, axis=-1)
    h = rms_norm(x, norm_g[l, 0]) * (1 + scale1) + shift1
    (aq, ak, av), (bq, bk, bv), (cq, ck, cv) = mixer_qkv(h, w_in[l], gqa_qk_norm[l], latent)
    lam_init = 0.8 - 0.6 * math.exp(-0.3 * l)
    lq = diff_lambda[l].astype(jnp.float32)
    lam = jnp.exp(jnp.sum(lq[0] * lq[1])) - jnp.exp(jnp.sum(lq[2] * lq[3])) + lam_init
    if latent:
        ka, va, kb, vb, kc, vc = ctx
        oa = neighbourhood_attend(aq, ak, av, ka, va, na_rpb[l])
        ob = diff_attend(bq, jnp.concatenate([bk, kb], axis=1), jnp.concatenate([bv, vb], axis=1), lam)
        oc = gqa_attend(cq, jnp.concatenate([ck, kc], axis=1), jnp.concatenate([cv, vc], axis=1))
    else:
        oa = gqa_attend(aq, ak, av)
        ob = diff_attend(bq, bk, bv, lam)
        oc = gqa_attend(cq, ck, cv)
    ob = (rms_norm(ob, diff_subln[l]) * (1 - lam_init)).reshape(B, T, H_B * DV_B)
    o = jnp.concatenate([oa, ob, oc], axis=-1) @ w_out[l]
    x = x + gate1 * o
    h2 = rms_norm(x, norm_g[l, 1]) * (1 + scale2) + shift2
    x = x + gate2 * peer_ffn(h2, peer_wq[l], peer_keys[l], peer_u[l], peer_v[l])
    return x, (ak, av, bk, bv, ck, cv)


def setup_inputs(seed: int = 0) -> dict:
    key = jax.random.key(seed)
    ks = jax.random.split(key, 24)
    f32 = jnp.float32

    def nrm(k, shape, s=1.0):
        return jax.random.normal(k, shape, f32) * s

    return {
        "x_prompt": nrm(ks[0], (BATCH, SEQ, D_MODEL)),
        "x_sample": nrm(ks[1], (DEC_BATCH, DEC_SEQ, D_MODEL)),
        "cache_a_k": nrm(ks[2], (DEC_BATCH, DEPTH, PAST_LEN, H_A, HEAD_DIM)),
        "cache_a_v": nrm(ks[3], (DEC_BATCH, DEPTH, PAST_LEN, H_A, HEAD_DIM)),
        "cache_b_k": nrm(ks[4], (DEC_BATCH, DEPTH, PAST_LEN, H_B, 2, DQK_B)),
        "cache_b_v": nrm(ks[5], (DEC_BATCH, DEPTH, PAST_LEN, H_B, DV_B)),
        "cache_c_k": nrm(ks[6], (DEC_BATCH, DEPTH, PAST_LEN, KV_C, HEAD_DIM)),
        "cache_c_v": nrm(ks[7], (DEC_BATCH, DEPTH, PAST_LEN, KV_C, HEAD_DIM)),
        "c": nrm(ks[8], (DEC_BATCH, D_MODEL)),
        "c_ctx": nrm(ks[9], (D_MODEL,)),
        "w_mod": nrm(ks[10], (DEPTH, D_MODEL, 6 * D_MODEL), 0.5 * D_MODEL ** -0.5),
        "b_mod": nrm(ks[11], (DEPTH, 6 * D_MODEL), 0.02),
        "norm_g": 1.0 + nrm(ks[12], (DEPTH, 2, D_MODEL), 0.1),
        "w_in": nrm(ks[13], (DEPTH, D_MODEL, IN_WIDTH), D_MODEL ** -0.5),
        "w_out": nrm(ks[14], (DEPTH, MIX_WIDTH, D_MODEL), MIX_WIDTH ** -0.5),
        "na_rpb": nrm(ks[15], (DEPTH, H_A, 2 * NA_ROWS - 1, 2 * NA_COLS - 1), 0.1),
        "diff_lambda": nrm(ks[16], (DEPTH, 4, DQK_B), 0.1),
        "diff_subln": 1.0 + nrm(ks[17], (DEPTH, DV_B), 0.1),
        "gqa_qk_norm": 1.0 + nrm(ks[18], (DEPTH, 2, HEAD_DIM), 0.1),
        "peer_wq": nrm(ks[19], (DEPTH, D_MODEL, PEER_HEADS * PEER_DQ), D_MODEL ** -0.5),
        "peer_keys": nrm(ks[20], (DEPTH, PEER_HEADS, 2, N_KEYS, PEER_DQ // 2), (PEER_DQ // 2) ** -0.5),
        "peer_u": nrm(ks[21], (DEPTH, N_EXPERTS, D_MODEL), D_MODEL ** -0.5),
        "peer_v": nrm(ks[22], (DEPTH, N_EXPERTS, D_MODEL), PEER_HEADS ** -0.5),
        "final_g": 1.0 + nrm(ks[23], (D_MODEL,), 0.1),
    }


def reference(x_prompt, x_sample, cache_a_k, cache_a_v, cache_b_k, cache_b_v, cache_c_k, cache_c_v,
              c, c_ctx, w_mod, b_mod, norm_g, w_in, w_out, na_rpb, diff_lambda, diff_subln,
              gqa_qk_norm, peer_wq, peer_keys, peer_u, peer_v, final_g):
    weights = (w_in, w_out, norm_g, na_rpb, diff_lambda, diff_subln, gqa_qk_norm,
               peer_wq, peer_keys, peer_u, peer_v)
    xp = x_prompt
    new_kv = []
    for l in range(DEPTH):
        mod_ctx = (jax.nn.silu(c_ctx) @ w_mod[l] + b_mod[l])[None, None, :]
        xp, kv = trunk_layer(xp, mod_ctx, l, None, *weights)
        new_kv.append(kv)
    y_prompt = rms_norm(xp, final_g)
    new_a_k = jnp.stack([kv[0] for kv in new_kv], axis=1)
    new_a_v = jnp.stack([kv[1] for kv in new_kv], axis=1)
    new_b_k = jnp.stack([kv[2] for kv in new_kv], axis=1)
    new_b_v = jnp.stack([kv[3] for kv in new_kv], axis=1)
    new_c_k = jnp.stack([kv[4] for kv in new_kv], axis=1)
    new_c_v = jnp.stack([kv[5] for kv in new_kv], axis=1)
    xs = x_sample
    for l in range(DEPTH):
        mod_lat = (jax.nn.silu(c) @ w_mod[l] + b_mod[l])[:, None, :]
        ctx = (cache_a_k[:, l], cache_a_v[:, l], cache_b_k[:, l], cache_b_v[:, l],
               cache_c_k[:, l], cache_c_v[:, l])
        xs, _ = trunk_layer(xs, mod_lat, l, ctx, *weights)
    y_sample = rms_norm(xs, final_g)
    return (y_prompt, y_sample, new_a_k, new_a_v, new_b_k, new_b_v, new_c_k, new_c_v)
```

```python
import functools
import math

import numpy as np
import jax
import jax.numpy as jnp
from jax import lax
from jax.experimental import pallas as pl
from jax.experimental.pallas import tpu as pltpu

F32 = jnp.float32
BF16 = jnp.bfloat16

D = 1024
BATCH, SEQ = 16, 256
DEPTH = 4
DEC_BATCH, DEC_SEQ = 4, 2048
PAST = 256
GRID_W = 64
ROWS = DEC_SEQ // GRID_W
HD = 64
H_A, H_B, H_C, KV_C = 6, 4, 6, 2
DQK_B, DV_B = 32, 64
NA_ROWS, NA_COLS = 8, 16
W_A = H_A * HD
W_BQK = H_B * 2 * DQK_B
W_BV = H_B * DV_B
W_CQ = H_C * HD
W_CKV = KV_C * HD
IN_WIDTH = 3 * W_A + 2 * W_BQK + W_BV + W_CQ + 2 * W_CKV
KV_WIDTH = 2 * W_A + W_BQK + W_BV + 2 * W_CKV
N_KEYS = 128
N_EXPERTS = N_KEYS * N_KEYS
PEER_HEADS, PEER_TOPK = 8, 16
ROPE_THETA = 10000.0
EPS = 1e-6

N_CTX = BATCH * SEQ
N_LAT = DEC_BATCH * DEC_SEQ
N_TOK = N_CTX + N_LAT

LANES = 128
TOK_TILE = 256
N_TILES = N_TOK // TOK_TILE
CTX_TILES = N_CTX // TOK_TILE
LAT_TILES_PER_BATCH = DEC_SEQ // TOK_TILE
NA_PAIR = 2 * GRID_W
NA_KEYS = 5 * NA_PAIR
TOPK_CHUNK = 128
EXP_TOK = 512
EXP_BLK = 1024
NEG = -1e30
VMEM_LIMIT = 56 * 1024 * 1024

_NT = (((1,), (1,)), ((), ()))


def _cparams(sem):
    return pltpu.CompilerParams(dimension_semantics=sem, vmem_limit_bytes=VMEM_LIMIT)


def _mod_row(i):
    return jnp.where(i < CTX_TILES, 0, 1 + (i - CTX_TILES) // LAT_TILES_PER_BATCH)


def _rms(x, g):
    return x * lax.rsqrt(jnp.mean(x * x, axis=-1, keepdims=True) + EPS) * g


def _group_mean_sq(x, width):
    r = lax.broadcasted_iota(jnp.int32, (LANES, LANES), 0) // width
    c = lax.broadcasted_iota(jnp.int32, (LANES, LANES), 1) // width
    bd = jnp.where(r == c, 1.0 / width, 0.0).astype(BF16)
    t = x * x
    hi = t.astype(BF16)
    lo = (t - hi.astype(F32)).astype(BF16)
    return (jnp.dot(hi, bd, preferred_element_type=F32)
            + jnp.dot(lo, bd, preferred_element_type=F32))


def _rope(x, cos, sin_signed, half):
    lane = lax.broadcasted_iota(jnp.int32, x.shape, 1)
    first = (lane % (2 * half)) < half
    rot = jnp.where(first, pltpu.roll(x, LANES - half, 1), pltpu.roll(x, half, 1))
    return x * cos + rot * sin_signed


def _mod_kernel(cond_ref, w_ref, b_ref, o_ref):
    c = cond_ref[...]
    s = c / (1.0 + jnp.exp(-c))
    o_ref[0] = jnp.dot(s.astype(BF16), w_ref[0].astype(BF16),
                       preferred_element_type=F32) + b_ref[0]


def _modulation(cond, w_mod, b_mod):
    ncol = 4
    cw = 6 * D // ncol
    return pl.pallas_call(
        _mod_kernel,
        out_shape=jax.ShapeDtypeStruct((DEPTH, 8, 6 * D), F32),
        grid=(DEPTH, ncol),
        in_specs=[pl.BlockSpec((8, D), lambda l, j: (0, 0)),
                  pl.BlockSpec((1, D, cw), lambda l, j: (l, 0, j)),
                  pl.BlockSpec((1, 1, cw), lambda l, j: (l, 0, j))],
        out_specs=pl.BlockSpec((1, 8, cw), lambda l, j: (l, 0, j)),
        compiler_params=_cparams(("arbitrary", "arbitrary")),
        name="modulation",
    )(cond, w_mod, b_mod.reshape(DEPTH, 1, 6 * D))


def _prep_kernel(u_ref, v_ref, u16_ref, vt_ref):
    u16_ref[0] = u_ref[0].astype(BF16)
    vt_ref[0] = v_ref[0].T.astype(BF16)


def _prep_experts(peer_u, peer_v):
    eb = 512
    return pl.pallas_call(
        _prep_kernel,
        out_shape=(jax.ShapeDtypeStruct((DEPTH, N_EXPERTS, D), BF16),
                   jax.ShapeDtypeStruct((DEPTH, D, N_EXPERTS), BF16)),
        grid=(DEPTH, N_EXPERTS // eb),
        in_specs=[pl.BlockSpec((1, eb, D), lambda l, j: (l, j, 0)),
                  pl.BlockSpec((1, eb, D), lambda l, j: (l, j, 0))],
        out_specs=(pl.BlockSpec((1, eb, D), lambda l, j: (l, j, 0)),
                   pl.BlockSpec((1, D, eb), lambda l, j: (l, 0, j))),
        compiler_params=_cparams(("arbitrary", "arbitrary")),
        name="prep_experts",
    )(peer_u, peer_v)


def _pre_kernel(x_ref, mod_ref, g_ref, w_ref, qkn_ref, cb_ref, sb_ref, cc_ref, sc_ref,
                aq_ref, ak_ref, av_ref, bq_ref, bk_ref, bv_ref, cq_ref, ck_ref, cv_ref, kv_ref):
    x = x_ref[...]
    mod = mod_ref[0]
    shift1, scale1 = mod[:, 0:D], mod[:, D:2 * D]
    h = _rms(x, g_ref[0]) * (1.0 + scale1) + shift1
    r = jnp.dot(h.astype(BF16), w_ref[0], preferred_element_type=F32)

    o = 0
    aq = r[:, o:o + W_A]; o += W_A
    ak = r[:, o:o + W_A]; o += W_A
    av = r[:, o:o + W_A]; o += W_A
    bq = r[:, o:o + W_BQK]; o += W_BQK
    bk = r[:, o:o + W_BQK]; o += W_BQK
    bv = r[:, o:o + W_BV]; o += W_BV
    cq = r[:, o:o + W_CQ]; o += W_CQ
    ck = r[:, o:o + W_CKV]; o += W_CKV
    cv = r[:, o:o + W_CKV]

    cb, sb, cc, sc = cb_ref[...], sb_ref[...], cc_ref[...], sc_ref[...]
    qkn = qkn_ref[0]

    def rope_b(t):
        return jnp.concatenate([_rope(t[:, s:s + LANES], cb, sb, DQK_B // 4)
                                for s in range(0, t.shape[1], LANES)], axis=1)

    def norm_rope_c(t, gain):
        outs = []
        for s in range(0, t.shape[1], LANES):
            ts = t[:, s:s + LANES]
            ts = ts * lax.rsqrt(_group_mean_sq(ts, HD) + EPS) * gain
            outs.append(_rope(ts, cc, sc, HD // 4))
        return jnp.concatenate(outs, axis=1)

    bq = rope_b(bq)
    bk = rope_b(bk)
    cq = norm_rope_c(cq, qkn[0:1])
    ck = norm_rope_c(ck, qkn[1:2])

    aq_ref[...] = aq.astype(BF16)
    ak_ref[...] = ak.astype(BF16)
    av_ref[...] = av.astype(BF16)
    bq_ref[...] = bq.astype(BF16)
    bk_ref[...] = bk.astype(BF16)
    bv_ref[...] = bv.astype(BF16)
    cq_ref[...] = cq.astype(BF16)
    ck_ref[...] = ck.astype(BF16)
    cv_ref[...] = cv.astype(BF16)
    kv_ref[...] = jnp.concatenate([ak, av, bk, bv, ck, cv], axis=1)


def _pre(l, x, mod, norm_g, w_in16, qkn, tabs):
    widths = (W_A, W_A, W_A, W_BQK, W_BQK, W_BV, W_CQ, W_CKV, W_CKV)
    rope_blk = lambda i: (jnp.where(i < CTX_TILES, 0, 1 + (i - CTX_TILES) % LAT_TILES_PER_BATCH), 0)
    row = lambda i: (i, 0)
    kv_row = lambda i: (jnp.minimum(i, CTX_TILES), 0)
    out_shape = tuple(jax.ShapeDtypeStruct((N_TOK, w), BF16) for w in widths) + (
        jax.ShapeDtypeStruct((N_CTX + TOK_TILE, KV_WIDTH), F32),)
    out_specs = tuple(pl.BlockSpec((TOK_TILE, w), row) for w in widths) + (
        pl.BlockSpec((TOK_TILE, KV_WIDTH), kv_row),)
    return pl.pallas_call(
        _pre_kernel,
        out_shape=out_shape,
        grid=(N_TILES,),
        in_specs=[pl.BlockSpec((TOK_TILE, D), row),
                  pl.BlockSpec((1, 1, 6 * D), lambda i: (_mod_row(i), 0, 0)),
                  pl.BlockSpec((1, 1, D), lambda i: (2 * l, 0, 0)),
                  pl.BlockSpec((1, D, IN_WIDTH), lambda i: (l, 0, 0)),
                  pl.BlockSpec((1, 2, LANES), lambda i: (l, 0, 0)),
                  pl.BlockSpec((TOK_TILE, LANES), rope_blk),
                  pl.BlockSpec((TOK_TILE, LANES), rope_blk),
                  pl.BlockSpec((TOK_TILE, LANES), rope_blk),
                  pl.BlockSpec((TOK_TILE, LANES), rope_blk)],
        out_specs=out_specs,
        compiler_params=_cparams(("arbitrary",)),
        name=f"pre_{l}",
    )(x, mod, norm_g, w_in16, qkn, *tabs)


def _lane_half(shape, half):
    lane = lax.broadcasted_iota(jnp.int32, shape, 1)
    return (lane // HD) == half


def _keep_lanes(x, lo, width):
    lane = lax.broadcasted_iota(jnp.int32, x.shape, 1)
    return jnp.where((lane >= lo) & (lane < lo + width), x, jnp.zeros_like(x))


def _attend(q, kvs, scale, bias=None):
    ss = []
    for i, (k, _) in enumerate(kvs):
        s = lax.dot_general(q, k, _NT, preferred_element_type=F32) * scale
        if i == 0 and bias is not None:
            s = s + bias
        ss.append(s)
    m = jnp.max(ss[0], axis=-1, keepdims=True)
    for s in ss[1:]:
        m = jnp.maximum(m, jnp.max(s, axis=-1, keepdims=True))
    o, den = None, None
    for s, (_, v) in zip(ss, kvs):
        p = jnp.exp(s - m)
        ps = jnp.sum(p, axis=-1, keepdims=True)
        po = jnp.dot(p.astype(BF16), v, preferred_element_type=F32)
        o = po if o is None else o + po
        den = ps if den is None else den + ps
    return o / den


def _diff_lambda(dl_ref, lam_init):
    dl = dl_ref[0]
    a = jnp.sum(dl[0:1] * dl[1:2], axis=-1, keepdims=True)
    b = jnp.sum(dl[2:3] * dl[3:4], axis=-1, keepdims=True)
    return jnp.exp(a) - jnp.exp(b) + lam_init


def _mixer_b(bq, kvs_of_slab, lam, sub, lam_init):
    scale = DQK_B ** -0.5
    slabs = []
    for s in range(W_BV // LANES):
        halves = []
        for half in range(2):
            h = 2 * s + half
            maps = []
            for c in range(2):
                hc = 2 * h + c
                q = _keep_lanes(bq[:, (hc // 4) * LANES:(hc // 4 + 1) * LANES], (hc % 4) * DQK_B, DQK_B)
                maps.append(_attend(q, kvs_of_slab(hc // 4, s), scale))
            halves.append(maps[0] - lam * maps[1])
        od = jnp.where(_lane_half(halves[0].shape, 0), halves[0], halves[1])
        od = od * lax.rsqrt(_group_mean_sq(od, DV_B) + EPS) * sub * (1.0 - lam_init)
        slabs.append(od)
    return jnp.concatenate(slabs, axis=1)


def _mixer_c(cq, kvs):
    slabs = []
    for s in range(W_CQ // LANES):
        qs = cq[:, s * LANES:(s + 1) * LANES]
        o0 = _attend(_keep_lanes(qs, 0, HD), kvs, HD ** -0.5)
        o1 = _attend(_keep_lanes(qs, HD, HD), kvs, HD ** -0.5)
        slabs.append(jnp.where(_lane_half(o0.shape, 0), o0, o1))
    return jnp.concatenate(slabs, axis=1)


def _ctx_attn_kernel(aq_ref, ak_ref, av_ref, bq_ref, bk_ref, bv_ref, cq_ref, ck_ref, cv_ref,
                     dl_ref, sub_ref, o_ref, *, lam_init):
    lam = _diff_lambda(dl_ref, lam_init)
    aq, ak, av = aq_ref[...], ak_ref[...], av_ref[...]
    slabs = []
    for s in range(W_A // LANES):
        sl = slice(s * LANES, (s + 1) * LANES)
        kvs = [(ak[:, sl], av[:, sl])]
        o0 = _attend(_keep_lanes(aq[:, sl], 0, HD), kvs, HD ** -0.5)
        o1 = _attend(_keep_lanes(aq[:, sl], HD, HD), kvs, HD ** -0.5)
        slabs.append(jnp.where(_lane_half(o0.shape, 0), o0, o1))
    oa = jnp.concatenate(slabs, axis=1)

    bk, bv = bk_ref[...], bv_ref[...]
    ob = _mixer_b(bq_ref[...],
                  lambda ks, vs: [(bk[:, ks * LANES:(ks + 1) * LANES], bv[:, vs * LANES:(vs + 1) * LANES])],
                  lam, sub_ref[0], lam_init)
    oc = _mixer_c(cq_ref[...], [(ck_ref[...], cv_ref[...])])
    o_ref[...] = jnp.concatenate([oa, ob, oc], axis=1).astype(BF16)


def _ctx_attention(l, qkv, diff_lambda, sub_tiled):
    widths = (W_A, W_A, W_A, W_BQK, W_BQK, W_BV, W_CQ, W_CKV, W_CKV)
    lam_init = 0.8 - 0.6 * math.exp(-0.3 * l)
    return pl.pallas_call(
        functools.partial(_ctx_attn_kernel, lam_init=lam_init),
        out_shape=jax.ShapeDtypeStruct((N_TOK, D), BF16),
        grid=(BATCH,),
        in_specs=[pl.BlockSpec((SEQ, w), lambda b: (b, 0)) for w in widths] + [
            pl.BlockSpec((1, 4, DQK_B), lambda b: (l, 0, 0)),
            pl.BlockSpec((1, 1, LANES), lambda b: (l, 0, 0))],
        out_specs=pl.BlockSpec((SEQ, D), lambda b: (b, 0)),
        compiler_params=_cparams(("arbitrary",)),
        name=f"ctx_attn_{l}",
    )(*qkv, diff_lambda, sub_tiled)


def _lat_attn_kernel(aq_ref, ak_ref, av_ref, cak_ref, cav_ref, bias0_ref, bias1_ref,
                     bq_ref, bk_ref, bv_ref, cbk_ref, cbv_ref,
                     cq_ref, ck_ref, cv_ref, cck_ref, ccv_ref,
                     dl_ref, sub_ref, o_in_ref, o_ref, *, lam_init):
    del o_in_ref
    qb = pl.program_id(1)
    lam = _diff_lambda(dl_ref, lam_init)

    cak = cak_ref[0, 0].astype(BF16)
    cav = cav_ref[0, 0].astype(BF16)
    rows = []
    for pidx, bias_ref in enumerate((bias0_ref, bias1_ref)):
        j = 2 * qb + pidx
        start = pl.multiple_of(jnp.clip(j - 2, 0, ROWS // 2 - 5) * NA_PAIR, NA_PAIR)
        kwin = ak_ref[pl.ds(start, NA_KEYS), :]
        vwin = av_ref[pl.ds(start, NA_KEYS), :]
        q = aq_ref[pidx * NA_PAIR:(pidx + 1) * NA_PAIR, :]
        slabs = []
        for s in range(W_A // LANES):
            sl = slice(s * LANES, (s + 1) * LANES)
            kvs = [(kwin[:, sl], vwin[:, sl]), (cak[:, sl], cav[:, sl])]
            o0 = _attend(_keep_lanes(q[:, sl], 0, HD), kvs, HD ** -0.5, bias_ref[0, 2 * s])
            o1 = _attend(_keep_lanes(q[:, sl], HD, HD), kvs, HD ** -0.5, bias_ref[0, 2 * s + 1])
            slabs.append(jnp.where(_lane_half(o0.shape, 0), o0, o1))
        rows.append(jnp.concatenate(slabs, axis=1))
    oa = jnp.concatenate(rows, axis=0)

    bk, bv = bk_ref[...], bv_ref[...]
    cbk = cbk_ref[0, 0].astype(BF16)
    cbv = cbv_ref[0, 0].astype(BF16)
    ob = _mixer_b(bq_ref[...],
                  lambda ks, vs: [(bk[:, ks * LANES:(ks + 1) * LANES], bv[:, vs * LANES:(vs + 1) * LANES]),
                                  (cbk[:, ks * LANES:(ks + 1) * LANES], cbv[:, vs * LANES:(vs + 1) * LANES])],
                  lam, sub_ref[0], lam_init)
    oc = _mixer_c(cq_ref[...], [(ck_ref[...], cv_ref[...]),
                                (cck_ref[0, 0].astype(BF16), ccv_ref[0, 0].astype(BF16))])
    o_ref[...] = jnp.concatenate([oa, ob, oc], axis=1).astype(BF16)


def _lat_attention(l, qkv, caches, bias0, bias1, diff_lambda, sub_tiled, o_ctx):
    aq, ak, av, bq, bk, bv, cq, ck, cv = qkv
    cak, cav, cbk, cbv, cck, ccv = caches
    lam_init = 0.8 - 0.6 * math.exp(-0.3 * l)
    nqb = DEC_SEQ // TOK_TILE
    lat0 = N_CTX // TOK_TILE
    qrow = lambda b, q: (lat0 + b * nqb + q, 0)
    krow = lambda b, q: (N_CTX // DEC_SEQ + b, 0)
    cache = lambda b, q: (b, l, 0, 0)
    nb = nqb - 1
    return pl.pallas_call(
        functools.partial(_lat_attn_kernel, lam_init=lam_init),
        out_shape=jax.ShapeDtypeStruct((N_TOK, D), BF16),
        grid=(DEC_BATCH, nqb),
        in_specs=[pl.BlockSpec((TOK_TILE, W_A), qrow),
                  pl.BlockSpec((DEC_SEQ, W_A), krow),
                  pl.BlockSpec((DEC_SEQ, W_A), krow),
                  pl.BlockSpec((1, 1, PAST, W_A), cache),
                  pl.BlockSpec((1, 1, PAST, W_A), cache),
                  pl.BlockSpec((1, H_A, NA_PAIR, NA_KEYS),
                               lambda b, q: (l * 3 + jnp.where(q == 0, 0, jnp.where(q == nb, 2, 1)), 0, 0, 0)),
                  pl.BlockSpec((1, H_A, NA_PAIR, NA_KEYS),
                               lambda b, q: (l * 3 + jnp.where(q == 0, 0, jnp.where(q == nb, 2, 1)), 0, 0, 0)),
                  pl.BlockSpec((TOK_TILE, W_BQK), qrow),
                  pl.BlockSpec((DEC_SEQ, W_BQK), krow),
                  pl.BlockSpec((DEC_SEQ, W_BV), krow),
                  pl.BlockSpec((1, 1, PAST, W_BQK), cache),
                  pl.BlockSpec((1, 1, PAST, W_BV), cache),
                  pl.BlockSpec((TOK_TILE, W_CQ), qrow),
                  pl.BlockSpec((DEC_SEQ, W_CKV), krow),
                  pl.BlockSpec((DEC_SEQ, W_CKV), krow),
                  pl.BlockSpec((1, 1, PAST, W_CKV), cache),
                  pl.BlockSpec((1, 1, PAST, W_CKV), cache),
                  pl.BlockSpec((1, 4, DQK_B), lambda b, q: (l, 0, 0)),
                  pl.BlockSpec((1, 1, LANES), lambda b, q: (l, 0, 0)),
                  pl.BlockSpec(memory_space=pl.ANY)],
        out_specs=pl.BlockSpec((TOK_TILE, D), qrow),
        input_output_aliases={19: 0},
        compiler_params=_cparams(("arbitrary", "arbitrary")),
        name=f"lat_attn_{l}",
    )(aq, ak, av, cak, cav, bias0, bias1, bq, bk, bv, cbk, cbv, cq, ck, cv, cck, ccv,
      diff_lambda, sub_tiled, o_ctx)


def _post_kernel(o_ref, x_ref, mod_ref, g_ref, w_ref, x1_ref, h2t_ref):
    mod = mod_ref[0]
    gate1, shift2, scale2 = mod[:, 2 * D:3 * D], mod[:, 3 * D:4 * D], mod[:, 4 * D:5 * D]
    x1 = x_ref[...] + gate1 * jnp.dot(o_ref[...], w_ref[0], preferred_element_type=F32)
    x1_ref[...] = x1
    h2 = _rms(x1, g_ref[0]) * (1.0 + scale2) + shift2
    h2t_ref[...] = h2.T.astype(BF16)


def _post(l, o16, x, mod, norm_g, w_out16):
    row = lambda i: (i, 0)
    return pl.pallas_call(
        _post_kernel,
        out_shape=(jax.ShapeDtypeStruct((N_TOK, D), F32),
                   jax.ShapeDtypeStruct((D, N_TOK), BF16)),
        grid=(N_TILES,),
        in_specs=[pl.BlockSpec((TOK_TILE, D), row),
                  pl.BlockSpec((TOK_TILE, D), row),
                  pl.BlockSpec((1, 1, 6 * D), lambda i: (_mod_row(i), 0, 0)),
                  pl.BlockSpec((1, 1, D), lambda i: (2 * l + 1, 0, 0)),
                  pl.BlockSpec((1, D, D), lambda i: (l, 0, 0))],
        out_specs=(pl.BlockSpec((TOK_TILE, D), row),
                   pl.BlockSpec((D, TOK_TILE), lambda i: (0, i))),
        compiler_params=_cparams(("arbitrary",)),
        name=f"post_{l}",
    )(o16, x, mod, norm_g, w_out16)


def _top16(x):
    kio = lax.broadcasted_iota(jnp.int32, x.shape, 0).astype(F32)
    rank = jnp.full(x.shape, 99.0, F32)
    vals = []
    for r in range(PEER_TOPK):
        m = jnp.max(x, axis=0, keepdims=True)
        kmin = jnp.min(jnp.where(x == m, kio, float(N_KEYS)), axis=0, keepdims=True)
        sel = kio == kmin
        rank = jnp.where(sel, float(r), rank)
        x = jnp.where(sel, -jnp.inf, x)
        vals.append(m)
    return rank, jnp.concatenate(vals, axis=0)


def _merge16(s1r, s2r):
    io = lax.broadcasted_iota(jnp.int32, s1r.shape, 0).astype(F32)
    cnt = jnp.zeros(s1r.shape, F32)
    head = s1r + s2r[0:1]
    top = head[0:1]
    z = jnp.zeros_like(top)
    for _ in range(PEER_TOPK):
        m = jnp.max(head, axis=0, keepdims=True)
        z = z + jnp.exp(m - top)
        row = jnp.min(jnp.where(head == m, io, 99.0), axis=0, keepdims=True)
        hit = io == row
        cnt = cnt + jnp.where(hit, 1.0, 0.0)
        nxt_idx = jnp.max(jnp.where(hit, cnt, 0.0), axis=0, keepdims=True)
        nxt = jnp.max(jnp.where(io == nxt_idx, s2r, -jnp.inf), axis=0, keepdims=True)
        head = jnp.where(hit, s1r + nxt, head)
    return cnt, z


def _topk_kernel(h2t_ref, wqt_ref, keys_ref, rank2_ref, e2_ref, lk_ref, cw_ref, st_ref):
    qt = jnp.dot(wqt_ref[0], h2t_ref[...], preferred_element_type=F32).astype(BF16)
    for hc in range(2 * PEER_HEADS):
        st_ref[hc] = jnp.dot(keys_ref[0, hc], qt[hc * N_KEYS:(hc + 1) * N_KEYS, :],
                             preferred_element_type=F32)

    def head_body(h, carry):
        s1 = st_ref[2 * h]
        s2 = st_ref[2 * h + 1]
        rank1, s1r = _top16(s1)
        rank2, s2r = _top16(s2)
        cnt, z = _merge16(s1r, s2r)
        lk = jnp.zeros_like(rank1)
        for r in range(PEER_TOPK):
            lk = jnp.where(rank1 == float(r), cnt[r:r + 1], lk)
        rank2_ref[h] = rank2
        e2_ref[h] = jnp.exp(s2 - s2r[0:1])
        lk_ref[h] = lk
        cw_ref[h] = jnp.exp(s1 - s1r[0:1]) / z
        return carry

    lax.fori_loop(0, PEER_HEADS, head_body, 0)


def _topk(l, h2t, wqt16, keys16):
    blk = pl.BlockSpec((PEER_HEADS, N_KEYS, TOPK_CHUNK), lambda i: (0, 0, i))
    shp = jax.ShapeDtypeStruct((PEER_HEADS, N_KEYS, N_TOK), F32)
    return pl.pallas_call(
        _topk_kernel,
        out_shape=(shp, shp, shp, shp),
        grid=(N_TOK // TOPK_CHUNK,),
        in_specs=[pl.BlockSpec((D, TOPK_CHUNK), lambda i: (0, i)),
                  pl.BlockSpec((1, 2 * PEER_HEADS * N_KEYS, D), lambda i: (l, 0, 0)),
                  pl.BlockSpec((1, 2 * PEER_HEADS, N_KEYS, N_KEYS), lambda i: (l, 0, 0, 0))],
        out_specs=(blk, blk, blk, blk),
        scratch_shapes=[pltpu.VMEM((2 * PEER_HEADS, N_KEYS, TOPK_CHUNK), F32)],
        compiler_params=_cparams(("arbitrary",)),
        name=f"topk_{l}",
    )(h2t, wqt16, keys16)


def _gelu(x):
    c = math.sqrt(2.0 / math.pi)
    return 0.5 * x * (1.0 + jnp.tanh(c * (x + 0.044715 * (x * x * x))))


def _expert_kernel(h2t_ref, u_ref, vt_ref, rank2_ref, e2_ref, lk_ref, cw_ref, x1_ref, mod_ref,
                   o_ref, acc_ref, ag_ref):
    j = pl.program_id(1)

    @pl.when(j == 0)
    def _():
        acc_ref[...] = jnp.zeros_like(acc_ref)

    at = jnp.dot(u_ref[0], h2t_ref[...], preferred_element_type=F32)
    for a in range(EXP_BLK // N_KEYS):
        g = None
        for h in range(PEER_HEADS):
            hit = rank2_ref[h] < lk_ref[h, a:a + 1, :]
            contrib = jnp.where(hit, e2_ref[h] * cw_ref[h, a:a + 1, :], 0.0)
            g = contrib if g is None else g + contrib
        ag_ref[a * N_KEYS:(a + 1) * N_KEYS, :] = (_gelu(at[a * N_KEYS:(a + 1) * N_KEYS, :]) * g).astype(BF16)
    acc_ref[...] += jnp.dot(vt_ref[0], ag_ref[...], preferred_element_type=F32)

    @pl.when(j == pl.num_programs(1) - 1)
    def _():
        gate2 = mod_ref[0][:, 5 * D:6 * D]
        o_ref[...] = x1_ref[...] + gate2 * acc_ref[...].T


def _experts(l, h2t, u16, vt16, rank2, e2, lk, cw, x1, mod):
    tiles_per_mod = TOK_TILE and (EXP_TOK // TOK_TILE)
    tok = pl.BlockSpec((PEER_HEADS, N_KEYS, EXP_TOK), lambda i, j: (0, 0, i))
    sel = pl.BlockSpec((PEER_HEADS, EXP_BLK // N_KEYS, EXP_TOK), lambda i, j: (0, j, i))
    return pl.pallas_call(
        _expert_kernel,
        out_shape=jax.ShapeDtypeStruct((N_TOK, D), F32),
        grid=(N_TOK // EXP_TOK, N_EXPERTS // EXP_BLK),
        in_specs=[pl.BlockSpec((D, EXP_TOK), lambda i, j: (0, i)),
                  pl.BlockSpec((1, EXP_BLK, D), lambda i, j: (l, j, 0)),
                  pl.BlockSpec((1, D, EXP_BLK), lambda i, j: (l, 0, j)),
                  tok, tok, sel, sel,
                  pl.BlockSpec((EXP_TOK, D), lambda i, j: (i, 0)),
                  pl.BlockSpec((1, 1, 6 * D), lambda i, j: (_mod_row(i * tiles_per_mod), 0, 0))],
        out_specs=pl.BlockSpec((EXP_TOK, D), lambda i, j: (i, 0)),
        scratch_shapes=[pltpu.VMEM((D, EXP_TOK), F32),
                        pltpu.VMEM((EXP_BLK, EXP_TOK), BF16)],
        compiler_params=_cparams(("arbitrary", "arbitrary")),
        name=f"experts_{l}",
    )(h2t, u16, vt16, rank2, e2, lk, cw, x1, mod)


def _final_kernel(x_ref, g_ref, o_ref):
    o_ref[...] = _rms(x_ref[...], g_ref[...])


def _final_norm(x, g):
    row = lambda i: (i, 0)
    return pl.pallas_call(
        _final_kernel,
        out_shape=jax.ShapeDtypeStruct((N_TOK, D), F32),
        grid=(N_TILES,),
        in_specs=[pl.BlockSpec((TOK_TILE, D), row), pl.BlockSpec((1, D), lambda i: (0, 0))],
        out_specs=pl.BlockSpec((TOK_TILE, D), row),
        compiler_params=_cparams(("arbitrary",)),
        name="final_norm",
    )(x, g.reshape(1, D))


def _rope_tables():
    t = jnp.arange(DEC_SEQ)
    rowp = (t // GRID_W).astype(F32)[:, None]
    colp = (t % GRID_W).astype(F32)[:, None]
    lane = np.arange(LANES)

    def tab(d):
        q = d // 4
        dd = lane % d
        inv = ROPE_THETA ** (-jnp.arange(q, dtype=F32) / q)
        ang = jnp.where((dd < d // 2)[None, :], rowp, colp) * inv[dd % q][None, :]
        sign = np.where((dd % (d // 2)) < q, -1.0, 1.0).astype(np.float32)[None, :]
        cos = jnp.concatenate([jnp.ones((TOK_TILE, LANES), F32), jnp.cos(ang)], axis=0)
        sin = jnp.concatenate([jnp.zeros((TOK_TILE, LANES), F32), jnp.sin(ang) * sign], axis=0)
        return cos, sin

    cb, sb = tab(DQK_B)
    cc, sc = tab(HD)
    return cb, sb, cc, sc


def _na_bias_index():
    def one(j):
        idx = np.zeros((NA_PAIR, NA_KEYS), np.int32)
        ok = np.zeros((NA_PAIR, NA_KEYS), bool)
        m0 = min(max(j - 2, 0), ROWS // 2 - 5)
        qi = np.arange(NA_PAIR)
        r = 2 * j + qi // GRID_W
        qc = qi % GRID_W
        rs = np.clip(r - NA_ROWS // 2, 0, ROWS - NA_ROWS)
        cs = np.clip(qc - NA_COLS // 2, 0, GRID_W - NA_COLS)
        ki = np.arange(NA_KEYS)
        krow = 2 * m0 + ki // GRID_W
        kc = ki % GRID_W
        row_ok = (krow[None, :] >= rs[:, None]) & (krow[None, :] < rs[:, None] + NA_ROWS)
        col_ok = (kc[None, :] >= cs[:, None]) & (kc[None, :] < cs[:, None] + NA_COLS)
        dr = np.clip(krow[None, :] - r[:, None] + NA_ROWS - 1, 0, 2 * NA_ROWS - 2)
        dc = np.clip(kc[None, :] - qc[:, None], -(NA_COLS - 1), NA_COLS - 1) + NA_COLS - 1
        idx[...] = dr * (2 * NA_COLS - 1) + dc
        ok[...] = row_ok & col_ok
        return idx, ok

    mid = ROWS // 4
    first = [one(j) for j in (0, mid, ROWS // 2 - 2)]
    second = [one(j) for j in (1, mid, ROWS // 2 - 1)]
    stack = lambda xs, k: np.stack([x[k] for x in xs])
    return (stack(first, 0), stack(first, 1)), (stack(second, 0), stack(second, 1))


def _na_bias(na_rpb, idx, ok):
    flat = na_rpb.reshape(DEPTH, H_A, -1)
    b = jnp.where(ok[None, None], flat[:, :, idx], NEG)
    return b.transpose(0, 2, 1, 3, 4).reshape(DEPTH * 3, H_A, NA_PAIR, NA_KEYS)


def kernel(x_prompt, x_sample, cache_a_k, cache_a_v, cache_b_k, cache_b_v, cache_c_k, cache_c_v,
           c, c_ctx, w_mod, b_mod, norm_g, w_in, w_out, na_rpb, diff_lambda, diff_subln,
           gqa_qk_norm, peer_wq, peer_keys, peer_u, peer_v, final_g):
    cperm = np.array([0, 3, 1, 4, 2, 5])
    cq0 = 3 * W_A + 2 * W_BQK + W_BV
    col = np.arange(IN_WIDTH)
    col[cq0:cq0 + W_CQ] = cq0 + (cperm[:, None] * HD + np.arange(HD)[None, :]).reshape(-1)
    w_in16 = w_in[:, :, col].astype(BF16)
    orow = np.arange(D)
    oc0 = W_A + W_BV
    orow[oc0:] = oc0 + (cperm[:, None] * HD + np.arange(HD)[None, :]).reshape(-1)
    w_out16 = w_out[:, orow, :].astype(BF16)
    wqt16 = peer_wq.transpose(0, 2, 1).astype(BF16)
    keys16 = peer_keys.reshape(DEPTH, 2 * PEER_HEADS, N_KEYS, N_KEYS).astype(BF16)
    u16, vt16 = _prep_experts(peer_u, peer_v)

    cond = jnp.concatenate([c_ctx[None, :], c, jnp.zeros((3, D), F32)], axis=0)
    mod_all = _modulation(cond, w_mod, b_mod)[:, :1 + DEC_BATCH].reshape(DEPTH, 1 + DEC_BATCH, 1, 6 * D)
    norm_g3 = norm_g.reshape(2 * DEPTH, 1, D)
    qkn = jnp.tile(gqa_qk_norm, (1, 1, LANES // HD))
    sub_tiled = jnp.tile(diff_subln, (1, LANES // DV_B)).reshape(DEPTH, 1, LANES)
    tabs = _rope_tables()
    (idx0, ok0), (idx1, ok1) = _na_bias_index()
    bias0 = _na_bias(na_rpb, idx0, ok0)
    bias1 = _na_bias(na_rpb, idx1, ok1)
    caches = (cache_a_k.reshape(DEC_BATCH, DEPTH, PAST, W_A), cache_a_v.reshape(DEC_BATCH, DEPTH, PAST, W_A),
              cache_b_k.reshape(DEC_BATCH, DEPTH, PAST, W_BQK), cache_b_v.reshape(DEC_BATCH, DEPTH, PAST, W_BV),
              cache_c_k.reshape(DEC_BATCH, DEPTH, PAST, W_CKV), cache_c_v.reshape(DEC_BATCH, DEPTH, PAST, W_CKV))

    x = jnp.concatenate([x_prompt.reshape(N_CTX, D), x_sample.reshape(N_LAT, D)], axis=0)
    new_kv = []
    for l in range(DEPTH):
        mod = mod_all[l]
        *qkv, kv = _pre(l, x, mod, norm_g3, w_in16, qkn, tabs)
        new_kv.append(kv[:N_CTX].reshape(BATCH, SEQ, KV_WIDTH))
        o16 = _ctx_attention(l, qkv, diff_lambda, sub_tiled)
        o16 = _lat_attention(l, qkv, caches, bias0, bias1, diff_lambda, sub_tiled, o16)
        x1, h2t = _post(l, o16, x, mod, norm_g3, w_out16)
        rank2, e2, lk, cw = _topk(l, h2t, wqt16, keys16)
        x = _experts(l, h2t, u16, vt16, rank2, e2, lk, cw, x1, mod)
    y = _final_norm(x, final_g)

    kv = jnp.stack(new_kv, axis=1)
    o = 0
    parts = []
    for w, shp in ((W_A, (H_A, HD)), (W_A, (H_A, HD)), (W_BQK, (H_B, 2, DQK_B)),
                   (W_BV, (H_B, DV_B)), (W_CKV, (KV_C, HD)), (W_CKV, (KV_C, HD))):
        parts.append(kv[..., o:o + w].reshape(BATCH, DEPTH, SEQ, *shp))
        o += w
    return (y[:N_CTX].reshape(BATCH, SEQ, D), y[N_CTX:].reshape(DEC_BATCH, DEC_SEQ, D), *parts)
```

```python
import functools
import math

import numpy as np
import jax
import jax.numpy as jnp
from jax import lax
from jax.experimental import pallas as pl
from jax.experimental.pallas import tpu as pltpu

F32 = jnp.float32
BF16 = jnp.bfloat16

D = 1024
BATCH, SEQ = 16, 256
DEPTH = 4
DEC_BATCH, DEC_SEQ = 4, 2048
PAST = 256
GRID_W = 64
ROWS = DEC_SEQ // GRID_W
HD = 64
H_A, H_B, H_C, KV_C = 6, 4, 6, 2
DQK_B, DV_B = 32, 64
NA_ROWS, NA_COLS = 8, 16
W_A = H_A * HD
W_BQK = H_B * 2 * DQK_B
W_BV = H_B * DV_B
W_CQ = H_C * HD
W_CKV = KV_C * HD
IN_WIDTH = 3 * W_A + 2 * W_BQK + W_BV + W_CQ + 2 * W_CKV
KV_WIDTH = 2 * W_A + W_BQK + W_BV + 2 * W_CKV
N_KEYS = 128
N_EXPERTS = N_KEYS * N_KEYS
PEER_HEADS, PEER_TOPK = 8, 16
ROPE_THETA = 10000.0
EPS = 1e-6

N_CTX = BATCH * SEQ
N_LAT = DEC_BATCH * DEC_SEQ
N_TOK = N_CTX + N_LAT

LANES = 128
TOK_TILE = 256
N_TILES = N_TOK // TOK_TILE
CTX_TILES = N_CTX // TOK_TILE
LAT_TILES_PER_BATCH = DEC_SEQ // TOK_TILE
NA_PAIR = 2 * GRID_W
NA_KEYS = 5 * NA_PAIR
TOPK_CHUNK = 128
EXP_TOK = 512
EXP_BLK = 1024
EXP_CHUNK = 256
NEG = -1e30
VMEM_LIMIT = 56 * 1024 * 1024

_NT = (((1,), (1,)), ((), ()))


def _cparams(sem):
    return pltpu.CompilerParams(dimension_semantics=sem, vmem_limit_bytes=VMEM_LIMIT)


def _mod_row(i):
    return jnp.where(i < CTX_TILES, 0, 1 + (i - CTX_TILES) // LAT_TILES_PER_BATCH)


def _rms(x, g):
    return x * lax.rsqrt(jnp.mean(x * x, axis=-1, keepdims=True) + EPS) * g


def _group_mean_sq(x, width):
    r = lax.broadcasted_iota(jnp.int32, (LANES, LANES), 0) // width
    c = lax.broadcasted_iota(jnp.int32, (LANES, LANES), 1) // width
    bd = jnp.where(r == c, 1.0 / width, 0.0).astype(BF16)
    t = x * x
    hi = t.astype(BF16)
    lo = (t - hi.astype(F32)).astype(BF16)
    return (jnp.dot(hi, bd, preferred_element_type=F32)
            + jnp.dot(lo, bd, preferred_element_type=F32))


def _rope(x, cos, sin_signed, half):
    lane = lax.broadcasted_iota(jnp.int32, x.shape, 1)
    first = (lane % (2 * half)) < half
    rot = jnp.where(first, pltpu.roll(x, LANES - half, 1), pltpu.roll(x, half, 1))
    return x * cos + rot * sin_signed


def _mod_kernel(cond_ref, w_ref, b_ref, o_ref):
    c = cond_ref[...]
    s = c / (1.0 + jnp.exp(-c))
    o_ref[0] = jnp.dot(s.astype(BF16), w_ref[0].astype(BF16),
                       preferred_element_type=F32) + b_ref[0]


def _modulation(cond, w_mod, b_mod):
    ncol = 4
    cw = 6 * D // ncol
    return pl.pallas_call(
        _mod_kernel,
        out_shape=jax.ShapeDtypeStruct((DEPTH, 8, 6 * D), F32),
        grid=(DEPTH, ncol),
        in_specs=[pl.BlockSpec((8, D), lambda l, j: (0, 0)),
                  pl.BlockSpec((1, D, cw), lambda l, j: (l, 0, j)),
                  pl.BlockSpec((1, 1, cw), lambda l, j: (l, 0, j))],
        out_specs=pl.BlockSpec((1, 8, cw), lambda l, j: (l, 0, j)),
        compiler_params=_cparams(("arbitrary", "arbitrary")),
        name="modulation",
    )(cond, w_mod, b_mod.reshape(DEPTH, 1, 6 * D))


def _prep_kernel(u_ref, v_ref, u16_ref, vt_ref):
    u16_ref[0] = u_ref[0].astype(BF16)
    vt_ref[0] = v_ref[0].T.astype(BF16)


def _prep_experts(peer_u, peer_v):
    eb = 512
    return pl.pallas_call(
        _prep_kernel,
        out_shape=(jax.ShapeDtypeStruct((DEPTH, N_EXPERTS, D), BF16),
                   jax.ShapeDtypeStruct((DEPTH, D, N_EXPERTS), BF16)),
        grid=(DEPTH, N_EXPERTS // eb),
        in_specs=[pl.BlockSpec((1, eb, D), lambda l, j: (l, j, 0)),
                  pl.BlockSpec((1, eb, D), lambda l, j: (l, j, 0))],
        out_specs=(pl.BlockSpec((1, eb, D), lambda l, j: (l, j, 0)),
                   pl.BlockSpec((1, D, eb), lambda l, j: (l, 0, j))),
        compiler_params=_cparams(("arbitrary", "arbitrary")),
        name="prep_experts",
    )(peer_u, peer_v)


def _pre_kernel(x_ref, mod_ref, g_ref, w_ref, qkn_ref, cb_ref, sb_ref, cc_ref, sc_ref,
                aq_ref, ak_ref, av_ref, bq_ref, bk_ref, bv_ref, cq_ref, ck_ref, cv_ref, kv_ref):
    x = x_ref[...]
    mod = mod_ref[0]
    shift1, scale1 = mod[:, 0:D], mod[:, D:2 * D]
    h = _rms(x, g_ref[0]) * (1.0 + scale1) + shift1
    r = jnp.dot(h.astype(BF16), w_ref[0], preferred_element_type=F32)

    o = 0
    aq = r[:, o:o + W_A]; o += W_A
    ak = r[:, o:o + W_A]; o += W_A
    av = r[:, o:o + W_A]; o += W_A
    bq = r[:, o:o + W_BQK]; o += W_BQK
    bk = r[:, o:o + W_BQK]; o += W_BQK
    bv = r[:, o:o + W_BV]; o += W_BV
    cq = r[:, o:o + W_CQ]; o += W_CQ
    ck = r[:, o:o + W_CKV]; o += W_CKV
    cv = r[:, o:o + W_CKV]

    cb, sb, cc, sc = cb_ref[...], sb_ref[...], cc_ref[...], sc_ref[...]
    qkn = qkn_ref[0]

    def rope_b(t):
        return jnp.concatenate([_rope(t[:, s:s + LANES], cb, sb, DQK_B // 4)
                                for s in range(0, t.shape[1], LANES)], axis=1)

    def norm_rope_c(t, gain):
        outs = []
        for s in range(0, t.shape[1], LANES):
            ts = t[:, s:s + LANES]
            ts = ts * lax.rsqrt(_group_mean_sq(ts, HD) + EPS) * gain
            outs.append(_rope(ts, cc, sc, HD // 4))
        return jnp.concatenate(outs, axis=1)

    bq = rope_b(bq)
    bk = rope_b(bk)
    cq = norm_rope_c(cq, qkn[0:1])
    ck = norm_rope_c(ck, qkn[1:2])

    aq_ref[...] = aq.astype(BF16)
    ak_ref[...] = ak.astype(BF16)
    av_ref[...] = av.astype(BF16)
    bq_ref[...] = bq.astype(BF16)
    bk_ref[...] = bk.astype(BF16)
    bv_ref[...] = bv.astype(BF16)
    cq_ref[...] = cq.astype(BF16)
    ck_ref[...] = ck.astype(BF16)
    cv_ref[...] = cv.astype(BF16)
    kv_ref[...] = jnp.concatenate([ak, av, bk, bv, ck, cv], axis=1)


def _pre(l, x, mod, norm_g, w_in16, qkn, tabs):
    widths = (W_A, W_A, W_A, W_BQK, W_BQK, W_BV, W_CQ, W_CKV, W_CKV)
    rope_blk = lambda i: (jnp.where(i < CTX_TILES, 0, 1 + (i - CTX_TILES) % LAT_TILES_PER_BATCH), 0)
    row = lambda i: (i, 0)
    kv_row = lambda i: (jnp.minimum(i, CTX_TILES), 0)
    out_shape = tuple(jax.ShapeDtypeStruct((N_TOK, w), BF16) for w in widths) + (
        jax.ShapeDtypeStruct((N_CTX + TOK_TILE, KV_WIDTH), F32),)
    out_specs = tuple(pl.BlockSpec((TOK_TILE, w), row) for w in widths) + (
        pl.BlockSpec((TOK_TILE, KV_WIDTH), kv_row),)
    return pl.pallas_call(
        _pre_kernel,
        out_shape=out_shape,
        grid=(N_TILES,),
        in_specs=[pl.BlockSpec((TOK_TILE, D), row),
                  pl.BlockSpec((1, 1, 6 * D), lambda i: (_mod_row(i), 0, 0)),
                  pl.BlockSpec((1, 1, D), lambda i: (2 * l, 0, 0)),
                  pl.BlockSpec((1, D, IN_WIDTH), lambda i: (l, 0, 0)),
                  pl.BlockSpec((1, 2, LANES), lambda i: (l, 0, 0)),
                  pl.BlockSpec((TOK_TILE, LANES), rope_blk),
                  pl.BlockSpec((TOK_TILE, LANES), rope_blk),
                  pl.BlockSpec((TOK_TILE, LANES), rope_blk),
                  pl.BlockSpec((TOK_TILE, LANES), rope_blk)],
        out_specs=out_specs,
        compiler_params=_cparams(("arbitrary",)),
        name=f"pre_{l}",
    )(x, mod, norm_g, w_in16, qkn, *tabs)


def _lane_half(shape, half):
    lane = lax.broadcasted_iota(jnp.int32, shape, 1)
    return (lane // HD) == half


def _keep_lanes(x, lo, width):
    lane = lax.broadcasted_iota(jnp.int32, x.shape, 1)
    return jnp.where((lane >= lo) & (lane < lo + width), x, jnp.zeros_like(x))


def _attend(q, kvs, scale, bias=None):
    ss = []
    for i, (k, _) in enumerate(kvs):
        s = lax.dot_general(q, k, _NT, preferred_element_type=F32) * scale
        if i == 0 and bias is not None:
            s = s + bias
        ss.append(s)
    m = jnp.max(ss[0], axis=-1, keepdims=True)
    for s in ss[1:]:
        m = jnp.maximum(m, jnp.max(s, axis=-1, keepdims=True))
    o, den = None, None
    for s, (_, v) in zip(ss, kvs):
        p = jnp.exp(s - m)
        ps = jnp.sum(p, axis=-1, keepdims=True)
        po = jnp.dot(p.astype(BF16), v, preferred_element_type=F32)
        o = po if o is None else o + po
        den = ps if den is None else den + ps
    return o / den


def _diff_lambda(dl_ref, lam_init):
    dl = dl_ref[0]
    a = jnp.sum(dl[0:1] * dl[1:2], axis=-1, keepdims=True)
    b = jnp.sum(dl[2:3] * dl[3:4], axis=-1, keepdims=True)
    return jnp.exp(a) - jnp.exp(b) + lam_init


def _mixer_b(bq, kvs_of_slab, lam, sub, lam_init):
    scale = DQK_B ** -0.5
    slabs = []
    for s in range(W_BV // LANES):
        halves = []
        for half in range(2):
            h = 2 * s + half
            maps = []
            for c in range(2):
                hc = 2 * h + c
                q = _keep_lanes(bq[:, (hc // 4) * LANES:(hc // 4 + 1) * LANES], (hc % 4) * DQK_B, DQK_B)
                maps.append(_attend(q, kvs_of_slab(hc // 4, s), scale))
            halves.append(maps[0] - lam * maps[1])
        od = jnp.where(_lane_half(halves[0].shape, 0), halves[0], halves[1])
        od = od * lax.rsqrt(_group_mean_sq(od, DV_B) + EPS) * sub * (1.0 - lam_init)
        slabs.append(od)
    return jnp.concatenate(slabs, axis=1)


def _mixer_c(cq, kvs):
    slabs = []
    for s in range(W_CQ // LANES):
        qs = cq[:, s * LANES:(s + 1) * LANES]
        o0 = _attend(_keep_lanes(qs, 0, HD), kvs, HD ** -0.5)
        o1 = _attend(_keep_lanes(qs, HD, HD), kvs, HD ** -0.5)
        slabs.append(jnp.where(_lane_half(o0.shape, 0), o0, o1))
    return jnp.concatenate(slabs, axis=1)


def _ctx_attn_kernel(aq_ref, ak_ref, av_ref, bq_ref, bk_ref, bv_ref, cq_ref, ck_ref, cv_ref,
                     dl_ref, sub_ref, o_ref, *, lam_init):
    lam = _diff_lambda(dl_ref, lam_init)
    aq, ak, av = aq_ref[...], ak_ref[...], av_ref[...]
    slabs = []
    for s in range(W_A // LANES):
        sl = slice(s * LANES, (s + 1) * LANES)
        kvs = [(ak[:, sl], av[:, sl])]
        o0 = _attend(_keep_lanes(aq[:, sl], 0, HD), kvs, HD ** -0.5)
        o1 = _attend(_keep_lanes(aq[:, sl], HD, HD), kvs, HD ** -0.5)
        slabs.append(jnp.where(_lane_half(o0.shape, 0), o0, o1))
    oa = jnp.concatenate(slabs, axis=1)

    bk, bv = bk_ref[...], bv_ref[...]
    ob = _mixer_b(bq_ref[...],
                  lambda ks, vs: [(bk[:, ks * LANES:(ks + 1) * LANES], bv[:, vs * LANES:(vs + 1) * LANES])],
                  lam, sub_ref[0], lam_init)
    oc = _mixer_c(cq_ref[...], [(ck_ref[...], cv_ref[...])])
    o_ref[...] = jnp.concatenate([oa, ob, oc], axis=1).astype(BF16)


def _ctx_attention(l, qkv, diff_lambda, sub_tiled):
    widths = (W_A, W_A, W_A, W_BQK, W_BQK, W_BV, W_CQ, W_CKV, W_CKV)
    lam_init = 0.8 - 0.6 * math.exp(-0.3 * l)
    return pl.pallas_call(
        functools.partial(_ctx_attn_kernel, lam_init=lam_init),
        out_shape=jax.ShapeDtypeStruct((N_TOK, D), BF16),
        grid=(BATCH,),
        in_specs=[pl.BlockSpec((SEQ, w), lambda b: (b, 0)) for w in widths] + [
            pl.BlockSpec((1, 4, DQK_B), lambda b: (l, 0, 0)),
            pl.BlockSpec((1, 1, LANES), lambda b: (l, 0, 0))],
        out_specs=pl.BlockSpec((SEQ, D), lambda b: (b, 0)),
        compiler_params=_cparams(("arbitrary",)),
        name=f"ctx_attn_{l}",
    )(*qkv, diff_lambda, sub_tiled)


def _lat_attn_kernel(aq_ref, ak_ref, av_ref, cak_ref, cav_ref, bias0_ref, bias1_ref,
                     bq_ref, bk_ref, bv_ref, cbk_ref, cbv_ref,
                     cq_ref, ck_ref, cv_ref, cck_ref, ccv_ref,
                     dl_ref, sub_ref, o_in_ref, o_ref, *, lam_init):
    del o_in_ref
    qb = pl.program_id(1)
    lam = _diff_lambda(dl_ref, lam_init)

    cak = cak_ref[0, 0].astype(BF16)
    cav = cav_ref[0, 0].astype(BF16)
    rows = []
    for pidx, bias_ref in enumerate((bias0_ref, bias1_ref)):
        j = 2 * qb + pidx
        start = pl.multiple_of(jnp.clip(j - 2, 0, ROWS // 2 - 5) * NA_PAIR, NA_PAIR)
        kwin = ak_ref[pl.ds(start, NA_KEYS), :]
        vwin = av_ref[pl.ds(start, NA_KEYS), :]
        q = aq_ref[pidx * NA_PAIR:(pidx + 1) * NA_PAIR, :]
        slabs = []
        for s in range(W_A // LANES):
            sl = slice(s * LANES, (s + 1) * LANES)
            kvs = [(kwin[:, sl], vwin[:, sl]), (cak[:, sl], cav[:, sl])]
            o0 = _attend(_keep_lanes(q[:, sl], 0, HD), kvs, HD ** -0.5, bias_ref[0, 2 * s])
            o1 = _attend(_keep_lanes(q[:, sl], HD, HD), kvs, HD ** -0.5, bias_ref[0, 2 * s + 1])
            slabs.append(jnp.where(_lane_half(o0.shape, 0), o0, o1))
        rows.append(jnp.concatenate(slabs, axis=1))
    oa = jnp.concatenate(rows, axis=0)

    bk, bv = bk_ref[...], bv_ref[...]
    cbk = cbk_ref[0, 0].astype(BF16)
    cbv = cbv_ref[0, 0].astype(BF16)
    ob = _mixer_b(bq_ref[...],
                  lambda ks, vs: [(bk[:, ks * LANES:(ks + 1) * LANES], bv[:, vs * LANES:(vs + 1) * LANES]),
                                  (cbk[:, ks * LANES:(ks + 1) * LANES], cbv[:, vs * LANES:(vs + 1) * LANES])],
                  lam, sub_ref[0], lam_init)
    oc = _mixer_c(cq_ref[...], [(ck_ref[...], cv_ref[...]),
                                (cck_ref[0, 0].astype(BF16), ccv_ref[0, 0].astype(BF16))])
    o_ref[...] = jnp.concatenate([oa, ob, oc], axis=1).astype(BF16)


def _lat_attention(l, qkv, caches, bias0, bias1, diff_lambda, sub_tiled, o_ctx):
    aq, ak, av, bq, bk, bv, cq, ck, cv = qkv
    cak, cav, cbk, cbv, cck, ccv = caches
    lam_init = 0.8 - 0.6 * math.exp(-0.3 * l)
    nqb = DEC_SEQ // TOK_TILE
    lat0 = N_CTX // TOK_TILE
    qrow = lambda b, q: (lat0 + b * nqb + q, 0)
    krow = lambda b, q: (N_CTX // DEC_SEQ + b, 0)
    cache = lambda b, q: (b, l, 0, 0)
    nb = nqb - 1
    return pl.pallas_call(
        functools.partial(_lat_attn_kernel, lam_init=lam_init),
        out_shape=jax.ShapeDtypeStruct((N_TOK, D), BF16),
        grid=(DEC_BATCH, nqb),
        in_specs=[pl.BlockSpec((TOK_TILE, W_A), qrow),
                  pl.BlockSpec((DEC_SEQ, W_A), krow),
                  pl.BlockSpec((DEC_SEQ, W_A), krow),
                  pl.BlockSpec((1, 1, PAST, W_A), cache),
                  pl.BlockSpec((1, 1, PAST, W_A), cache),
                  pl.BlockSpec((1, H_A, NA_PAIR, NA_KEYS),
                               lambda b, q: (l * 3 + jnp.where(q == 0, 0, jnp.where(q == nb, 2, 1)), 0, 0, 0)),
                  pl.BlockSpec((1, H_A, NA_PAIR, NA_KEYS),
                               lambda b, q: (l * 3 + jnp.where(q == 0, 0, jnp.where(q == nb, 2, 1)), 0, 0, 0)),
                  pl.BlockSpec((TOK_TILE, W_BQK), qrow),
                  pl.BlockSpec((DEC_SEQ, W_BQK), krow),
                  pl.BlockSpec((DEC_SEQ, W_BV), krow),
                  pl.BlockSpec((1, 1, PAST, W_BQK), cache),
                  pl.BlockSpec((1, 1, PAST, W_BV), cache),
                  pl.BlockSpec((TOK_TILE, W_CQ), qrow),
                  pl.BlockSpec((DEC_SEQ, W_CKV), krow),
                  pl.BlockSpec((DEC_SEQ, W_CKV), krow),
                  pl.BlockSpec((1, 1, PAST, W_CKV), cache),
                  pl.BlockSpec((1, 1, PAST, W_CKV), cache),
                  pl.BlockSpec((1, 4, DQK_B), lambda b, q: (l, 0, 0)),
                  pl.BlockSpec((1, 1, LANES), lambda b, q: (l, 0, 0)),
                  pl.BlockSpec(memory_space=pl.ANY)],
        out_specs=pl.BlockSpec((TOK_TILE, D), qrow),
        input_output_aliases={19: 0},
        compiler_params=_cparams(("arbitrary", "arbitrary")),
        name=f"lat_attn_{l}",
    )(aq, ak, av, cak, cav, bias0, bias1, bq, bk, bv, cbk, cbv, cq, ck, cv, cck, ccv,
      diff_lambda, sub_tiled, o_ctx)


def _post_kernel(o_ref, x_ref, mod_ref, g_ref, w_ref, x1_ref, h2t_ref):
    mod = mod_ref[0]
    gate1, shift2, scale2 = mod[:, 2 * D:3 * D], mod[:, 3 * D:4 * D], mod[:, 4 * D:5 * D]
    x1 = x_ref[...] + gate1 * jnp.dot(o_ref[...], w_ref[0], preferred_element_type=F32)
    x1_ref[...] = x1
    h2 = _rms(x1, g_ref[0]) * (1.0 + scale2) + shift2
    h2t_ref[...] = h2.T.astype(BF16)


def _post(l, o16, x, mod, norm_g, w_out16):
    row = lambda i: (i, 0)
    return pl.pallas_call(
        _post_kernel,
        out_shape=(jax.ShapeDtypeStruct((N_TOK, D), F32),
                   jax.ShapeDtypeStruct((D, N_TOK), BF16)),
        grid=(N_TILES,),
        in_specs=[pl.BlockSpec((TOK_TILE, D), row),
                  pl.BlockSpec((TOK_TILE, D), row),
                  pl.BlockSpec((1, 1, 6 * D), lambda i: (_mod_row(i), 0, 0)),
                  pl.BlockSpec((1, 1, D), lambda i: (2 * l + 1, 0, 0)),
                  pl.BlockSpec((1, D, D), lambda i: (l, 0, 0))],
        out_specs=(pl.BlockSpec((TOK_TILE, D), row),
                   pl.BlockSpec((D, TOK_TILE), lambda i: (0, i))),
        compiler_params=_cparams(("arbitrary",)),
        name=f"post_{l}",
    )(o16, x, mod, norm_g, w_out16)


def _top16(x):
    kio = lax.broadcasted_iota(jnp.int32, x.shape, 0).astype(F32)
    rank = jnp.full(x.shape, 99.0, F32)
    vals = []
    for r in range(PEER_TOPK):
        m = jnp.max(x, axis=0, keepdims=True)
        kmin = jnp.min(jnp.where(x == m, kio, float(N_KEYS)), axis=0, keepdims=True)
        sel = kio == kmin
        rank = jnp.where(sel, float(r), rank)
        x = jnp.where(sel, -jnp.inf, x)
        vals.append(m)
    return rank, jnp.concatenate(vals, axis=0)


def _merge16(s1r, s2r):
    io = lax.broadcasted_iota(jnp.int32, s1r.shape, 0).astype(F32)
    cnt = jnp.zeros(s1r.shape, F32)
    head = s1r + s2r[0:1]
    top = head[0:1]
    z = jnp.zeros_like(top)
    for _ in range(PEER_TOPK):
        m = jnp.max(head, axis=0, keepdims=True)
        z = z + jnp.exp(m - top)
        row = jnp.min(jnp.where(head == m, io, 99.0), axis=0, keepdims=True)
        hit = io == row
        cnt = cnt + jnp.where(hit, 1.0, 0.0)
        nxt_idx = jnp.max(jnp.where(hit, cnt, 0.0), axis=0, keepdims=True)
        nxt = jnp.max(jnp.where(io == nxt_idx, s2r, -jnp.inf), axis=0, keepdims=True)
        head = jnp.where(hit, s1r + nxt, head)
    return cnt, z


def _topk_kernel(h2t_ref, wqt_ref, keys_ref, rank2_ref, e2_ref, lk_ref, cw_ref, st_ref):
    qt = jnp.dot(wqt_ref[0], h2t_ref[...], preferred_element_type=F32).astype(BF16)
    for hc in range(2 * PEER_HEADS):
        st_ref[hc] = jnp.dot(keys_ref[0, hc], qt[hc * N_KEYS:(hc + 1) * N_KEYS, :],
                             preferred_element_type=F32)

    def head_body(h, carry):
        s1 = st_ref[2 * h]
        s2 = st_ref[2 * h + 1]
        rank1, s1r = _top16(s1)
        rank2, s2r = _top16(s2)
        cnt, z = _merge16(s1r, s2r)
        lk = jnp.zeros_like(rank1)
        for r in range(PEER_TOPK):
            lk = jnp.where(rank1 == float(r), cnt[r:r + 1], lk)
        rank2_ref[h] = rank2.astype(BF16)
        e2_ref[h] = jnp.exp(s2 - s2r[0:1]).astype(BF16)
        lk_ref[h] = lk
        cw_ref[h] = jnp.exp(s1 - s1r[0:1]) / z
        return carry

    lax.fori_loop(0, PEER_HEADS, head_body, 0)


def _topk(l, h2t, wqt16, keys16):
    blk = pl.BlockSpec((PEER_HEADS, N_KEYS, TOPK_CHUNK), lambda i: (0, 0, i))
    shp = jax.ShapeDtypeStruct((PEER_HEADS, N_KEYS, N_TOK), F32)
    shp16 = jax.ShapeDtypeStruct((PEER_HEADS, N_KEYS, N_TOK), BF16)
    return pl.pallas_call(
        _topk_kernel,
        out_shape=(shp16, shp16, shp, shp),
        grid=(N_TOK // TOPK_CHUNK,),
        in_specs=[pl.BlockSpec((D, TOPK_CHUNK), lambda i: (0, i)),
                  pl.BlockSpec((1, 2 * PEER_HEADS * N_KEYS, D), lambda i: (l, 0, 0)),
                  pl.BlockSpec((1, 2 * PEER_HEADS, N_KEYS, N_KEYS), lambda i: (l, 0, 0, 0))],
        out_specs=(blk, blk, blk, blk),
        scratch_shapes=[pltpu.VMEM((2 * PEER_HEADS, N_KEYS, TOPK_CHUNK), F32)],
        compiler_params=_cparams(("arbitrary",)),
        name=f"topk_{l}",
    )(h2t, wqt16, keys16)


def _gelu(x):
    c = math.sqrt(2.0 / math.pi)
    return 0.5 * x * (1.0 + jnp.tanh(c * (x + 0.044715 * (x * x * x))))


def _expert_kernel(h2t_ref, u_ref, vt_ref, rank2_ref, e2_ref, lk_ref, cw_ref, x1_ref, mod_ref,
                   o_ref, acc_ref, at0_ref, at1_ref):
    j = pl.program_id(1)
    nblk = pl.num_programs(1) - 1

    @pl.when(j == 0)
    def _():
        acc_ref[...] = jnp.zeros_like(acc_ref)
        at1_ref[...] = jnp.zeros_like(at1_ref)

    def step(at_w, at_r):
        at_w[...] = jnp.dot(u_ref[0], h2t_ref[...], preferred_element_type=F32)
        part = None
        per = EXP_CHUNK // N_KEYS
        for c in range(EXP_BLK // EXP_CHUNK):
            ags = []
            for a in range(c * per, (c + 1) * per):
                g = None
                for h in range(PEER_HEADS):
                    lk = lk_ref[h, a:a + 1, :].astype(BF16)
                    cw = cw_ref[h, a:a + 1, :].astype(BF16)
                    contrib = jnp.where(rank2_ref[h] < lk, e2_ref[h] * cw, jnp.zeros((), BF16))
                    g = contrib if g is None else g + contrib
                ags.append(_gelu(at_r[a * N_KEYS:(a + 1) * N_KEYS, :]).astype(BF16) * g)
            d = jnp.dot(vt_ref[0, :, c * EXP_CHUNK:(c + 1) * EXP_CHUNK], jnp.concatenate(ags, axis=0),
                        preferred_element_type=F32)
            part = d if part is None else part + d
        acc_ref[...] += part

    @pl.when(j % 2 == 0)
    def _():
        step(at0_ref, at1_ref)

    @pl.when(j % 2 == 1)
    def _():
        step(at1_ref, at0_ref)

    @pl.when(j == nblk)
    def _():
        gate2 = mod_ref[0][:, 5 * D:6 * D]
        o_ref[...] = x1_ref[...] + gate2 * acc_ref[...].T


def _experts(l, h2t, u16, vt16, rank2, e2, lk, cw, x1, mod):
    nblk = N_EXPERTS // EXP_BLK
    tok = pl.BlockSpec((PEER_HEADS, N_KEYS, EXP_TOK), lambda i, j: (0, 0, i))
    sel = pl.BlockSpec((PEER_HEADS, EXP_BLK // N_KEYS, EXP_TOK), lambda i, j: (0, jnp.maximum(j - 1, 0), i))
    return pl.pallas_call(
        _expert_kernel,
        out_shape=jax.ShapeDtypeStruct((N_TOK, D), F32),
        grid=(N_TOK // EXP_TOK, nblk + 1),
        in_specs=[pl.BlockSpec((D, EXP_TOK), lambda i, j: (0, i)),
                  pl.BlockSpec((1, EXP_BLK, D), lambda i, j: (l, jnp.minimum(j, nblk - 1), 0)),
                  pl.BlockSpec((1, D, EXP_BLK), lambda i, j: (l, 0, jnp.maximum(j - 1, 0))),
                  tok, tok, sel, sel,
                  pl.BlockSpec((EXP_TOK, D), lambda i, j: (i, 0)),
                  pl.BlockSpec((1, 1, 6 * D), lambda i, j: (_mod_row(i * (EXP_TOK // TOK_TILE)), 0, 0))],
        out_specs=pl.BlockSpec((EXP_TOK, D), lambda i, j: (i, 0)),
        scratch_shapes=[pltpu.VMEM((D, EXP_TOK), F32),
                        pltpu.VMEM((EXP_BLK, EXP_TOK), F32),
                        pltpu.VMEM((EXP_BLK, EXP_TOK), F32)],
        compiler_params=_cparams(("arbitrary", "arbitrary")),
        name=f"experts_{l}",
    )(h2t, u16, vt16, rank2, e2, lk, cw, x1, mod)


def _final_kernel(x_ref, g_ref, o_ref):
    o_ref[...] = _rms(x_ref[...], g_ref[...])


def _final_norm(x, g):
    row = lambda i: (i, 0)
    return pl.pallas_call(
        _final_kernel,
        out_shape=jax.ShapeDtypeStruct((N_TOK, D), F32),
        grid=(N_TILES,),
        in_specs=[pl.BlockSpec((TOK_TILE, D), row), pl.BlockSpec((1, D), lambda i: (0, 0))],
        out_specs=pl.BlockSpec((TOK_TILE, D), row),
        compiler_params=_cparams(("arbitrary",)),
        name="final_norm",
    )(x, g.reshape(1, D))


def _rope_tables():
    t = jnp.arange(DEC_SEQ)
    rowp = (t // GRID_W).astype(F32)[:, None]
    colp = (t % GRID_W).astype(F32)[:, None]
    lane = np.arange(LANES)

    def tab(d):
        q = d // 4
        dd = lane % d
        inv = ROPE_THETA ** (-jnp.arange(q, dtype=F32) / q)
        ang = jnp.where((dd < d // 2)[None, :], rowp, colp) * inv[dd % q][None, :]
        sign = np.where((dd % (d // 2)) < q, -1.0, 1.0).astype(np.float32)[None, :]
        cos = jnp.concatenate([jnp.ones((TOK_TILE, LANES), F32), jnp.cos(ang)], axis=0)
        sin = jnp.concatenate([jnp.zeros((TOK_TILE, LANES), F32), jnp.sin(ang) * sign], axis=0)
        return cos, sin

    cb, sb = tab(DQK_B)
    cc, sc = tab(HD)
    return cb, sb, cc, sc


def _na_bias_index():
    def one(j):
        m0 = min(max(j - 2, 0), ROWS // 2 - 5)
        r = 2 * j + np.arange(2)
        rs = np.clip(r - NA_ROWS // 2, 0, ROWS - NA_ROWS)
        krow = 2 * m0 + np.arange(NA_KEYS // GRID_W)
        row_ok = (krow[None, :] >= rs[:, None]) & (krow[None, :] < rs[:, None] + NA_ROWS)
        dr = np.clip(krow[None, :] - r[:, None] + NA_ROWS - 1, 0, 2 * NA_ROWS - 2)
        qc = np.arange(GRID_W)
        cs = np.clip(qc - NA_COLS // 2, 0, GRID_W - NA_COLS)
        col_ok = (qc[None, :] >= cs[:, None]) & (qc[None, :] < cs[:, None] + NA_COLS)
        ok = row_ok[:, None, :, None] & col_ok[None, :, None, :]
        return dr, ok.reshape(NA_PAIR, NA_KEYS)

    mid = ROWS // 4
    first = [one(j) for j in (0, mid, ROWS // 2 - 2)]
    second = [one(j) for j in (1, mid, ROWS // 2 - 1)]
    stack = lambda xs, k: np.stack([x[k] for x in xs])
    return (stack(first, 0), stack(first, 1)), (stack(second, 0), stack(second, 1))


def _na_bias(na_rpb, dr, ok):
    qc = np.arange(GRID_W)
    dc = np.clip(qc[None, :] - qc[:, None], -(NA_COLS - 1), NA_COLS - 1) + NA_COLS - 1
    onehot = (dc[None] == np.arange(2 * NA_COLS - 1)[:, None, None]).astype(np.float32)
    rows = jnp.einsum('lhrd,dn->lhrn', na_rpb, onehot.reshape(2 * NA_COLS - 1, GRID_W * GRID_W),
                      precision=lax.Precision.HIGHEST)
    nkr = NA_KEYS // GRID_W
    t = rows[:, :, dr].reshape(DEPTH, H_A, 3, 2, nkr, GRID_W, GRID_W)
    t = t.transpose(0, 2, 1, 3, 5, 4, 6).reshape(DEPTH, 3, H_A, NA_PAIR, NA_KEYS)
    return jnp.where(ok[None, :, None], t, NEG).reshape(DEPTH * 3, H_A, NA_PAIR, NA_KEYS)


def kernel(x_prompt, x_sample, cache_a_k, cache_a_v, cache_b_k, cache_b_v, cache_c_k, cache_c_v,
           c, c_ctx, w_mod, b_mod, norm_g, w_in, w_out, na_rpb, diff_lambda, diff_subln,
           gqa_qk_norm, peer_wq, peer_keys, peer_u, peer_v, final_g):
    cq0 = 3 * W_A + 2 * W_BQK + W_BV
    g = H_C // KV_C
    w_cq = w_in[:, :, cq0:cq0 + W_CQ].reshape(DEPTH, D, KV_C, g, HD).transpose(0, 1, 3, 2, 4)
    w_in16 = jnp.concatenate([w_in[:, :, :cq0], w_cq.reshape(DEPTH, D, W_CQ), w_in[:, :, cq0 + W_CQ:]],
                             axis=2).astype(BF16)
    oc0 = W_A + W_BV
    w_oc = w_out[:, oc0:].reshape(DEPTH, KV_C, g, HD, D).transpose(0, 2, 1, 3, 4)
    w_out16 = jnp.concatenate([w_out[:, :oc0], w_oc.reshape(DEPTH, W_CQ, D)], axis=1).astype(BF16)
    wqt16 = peer_wq.transpose(0, 2, 1).astype(BF16)
    keys16 = peer_keys.reshape(DEPTH, 2 * PEER_HEADS, N_KEYS, N_KEYS).astype(BF16)
    u16, vt16 = _prep_experts(peer_u, peer_v)

    cond = jnp.concatenate([c_ctx[None, :], c, jnp.zeros((3, D), F32)], axis=0)
    mod_all = _modulation(cond, w_mod, b_mod)[:, :1 + DEC_BATCH].reshape(DEPTH, 1 + DEC_BATCH, 1, 6 * D)
    norm_g3 = norm_g.reshape(2 * DEPTH, 1, D)
    qkn = jnp.tile(gqa_qk_norm, (1, 1, LANES // HD))
    sub_tiled = jnp.tile(diff_subln, (1, LANES // DV_B)).reshape(DEPTH, 1, LANES)
    tabs = _rope_tables()
    (dr0, ok0), (dr1, ok1) = _na_bias_index()
    bias0 = _na_bias(na_rpb, dr0, ok0)
    bias1 = _na_bias(na_rpb, dr1, ok1)
    caches = (cache_a_k.reshape(DEC_BATCH, DEPTH, PAST, W_A), cache_a_v.reshape(DEC_BATCH, DEPTH, PAST, W_A),
              cache_b_k.reshape(DEC_BATCH, DEPTH, PAST, W_BQK), cache_b_v.reshape(DEC_BATCH, DEPTH, PAST, W_BV),
              cache_c_k.reshape(DEC_BATCH, DEPTH, PAST, W_CKV), cache_c_v.reshape(DEC_BATCH, DEPTH, PAST, W_CKV))

    x = jnp.concatenate([x_prompt.reshape(N_CTX, D), x_sample.reshape(N_LAT, D)], axis=0)
    new_kv = []
    for l in range(DEPTH):
        mod = mod_all[l]
        *qkv, kv = _pre(l, x, mod, norm_g3, w_in16, qkn, tabs)
        new_kv.append(kv[:N_CTX].reshape(BATCH, SEQ, KV_WIDTH))
        o16 = _ctx_attention(l, qkv, diff_lambda, sub_tiled)
        o16 = _lat_attention(l, qkv, caches, bias0, bias1, diff_lambda, sub_tiled, o16)
        x1, h2t = _post(l, o16, x, mod, norm_g3, w_out16)
        rank2, e2, lk, cw = _topk(l, h2t, wqt16, keys16)
        x = _experts(l, h2t, u16, vt16, rank2, e2, lk, cw, x1, mod)
    y = _final_norm(x, final_g)

    kv = jnp.stack(new_kv, axis=1)
    o = 0
    parts = []
    for w, shp in ((W_A, (H_A, HD)), (W_A, (H_A, HD)), (W_BQK, (H_B, 2, DQK_B)),
                   (W_BV, (H_B, DV_B)), (W_CKV, (KV_C, HD)), (W_CKV, (KV_C, HD))):
        parts.append(kv[..., o:o + w].reshape(BATCH, DEPTH, SEQ, *shp))
        o += w
    return (y[:N_CTX].reshape(BATCH, SEQ, D), y[N_CTX:].reshape(DEC_BATCH, DEC_SEQ, D), *parts)
```

```python
import functools
import math

import numpy as np
import jax
import jax.numpy as jnp
from jax import lax
from jax.experimental import pallas as pl
from jax.experimental.pallas import tpu as pltpu

F32 = jnp.float32
BF16 = jnp.bfloat16

D = 1024
BATCH, SEQ = 16, 256
DEPTH = 4
DEC_BATCH, DEC_SEQ = 4, 2048
PAST = 256
GRID_W = 64
ROWS = DEC_SEQ // GRID_W
HD = 64
H_A, H_B, H_C, KV_C = 6, 4, 6, 2
DQK_B, DV_B = 32, 64
NA_ROWS, NA_COLS = 8, 16
W_A = H_A * HD
W_BQK = H_B * 2 * DQK_B
W_BV = H_B * DV_B
W_CQ = H_C * HD
W_CKV = KV_C * HD
IN_WIDTH = 3 * W_A + 2 * W_BQK + W_BV + W_CQ + 2 * W_CKV
KV_WIDTH = 2 * W_A + W_BQK + W_BV + 2 * W_CKV
N_KEYS = 128
N_EXPERTS = N_KEYS * N_KEYS
PEER_HEADS, PEER_TOPK = 8, 16
ROPE_THETA = 10000.0
EPS = 1e-6

N_CTX = BATCH * SEQ
N_LAT = DEC_BATCH * DEC_SEQ
N_TOK = N_CTX + N_LAT

LANES = 128
TOK_TILE = 256
N_TILES = N_TOK // TOK_TILE
CTX_TILES = N_CTX // TOK_TILE
LAT_TILES_PER_BATCH = DEC_SEQ // TOK_TILE
NA_PAIR = 2 * GRID_W
NA_KEYS = 5 * NA_PAIR
TOPK_CHUNK = 128
EXP_TOK = 512
EXP_BLK = 1024
EXP_CHUNK = 256
NEG = -1e30
VMEM_LIMIT = 56 * 1024 * 1024

_NT = (((1,), (1,)), ((), ()))


def _cparams(sem):
    return pltpu.CompilerParams(dimension_semantics=sem, vmem_limit_bytes=VMEM_LIMIT)


def _mod_row(i):
    return jnp.where(i < CTX_TILES, 0, 1 + (i - CTX_TILES) // LAT_TILES_PER_BATCH)


def _rms(x, g):
    return x * lax.rsqrt(jnp.mean(x * x, axis=-1, keepdims=True) + EPS) * g


def _group_mean_sq(x, width):
    r = lax.broadcasted_iota(jnp.int32, (LANES, LANES), 0) // width
    c = lax.broadcasted_iota(jnp.int32, (LANES, LANES), 1) // width
    bd = jnp.where(r == c, 1.0 / width, 0.0).astype(BF16)
    t = x * x
    hi = t.astype(BF16)
    lo = (t - hi.astype(F32)).astype(BF16)
    return (jnp.dot(hi, bd, preferred_element_type=F32)
            + jnp.dot(lo, bd, preferred_element_type=F32))


def _rope(x, cos, sin_signed, half):
    lane = lax.broadcasted_iota(jnp.int32, x.shape, 1)
    first = (lane % (2 * half)) < half
    rot = jnp.where(first, pltpu.roll(x, LANES - half, 1), pltpu.roll(x, half, 1))
    return x * cos + rot * sin_signed


def _mod_kernel(cond_ref, w_ref, b_ref, o_ref):
    c = cond_ref[...]
    s = c / (1.0 + jnp.exp(-c))
    o_ref[0] = jnp.dot(s.astype(BF16), w_ref[0].astype(BF16),
                       preferred_element_type=F32) + b_ref[0]


def _modulation(cond, w_mod, b_mod):
    ncol = 4
    cw = 6 * D // ncol
    return pl.pallas_call(
        _mod_kernel,
        out_shape=jax.ShapeDtypeStruct((DEPTH, 8, 6 * D), F32),
        grid=(DEPTH, ncol),
        in_specs=[pl.BlockSpec((8, D), lambda l, j: (0, 0)),
                  pl.BlockSpec((1, D, cw), lambda l, j: (l, 0, j)),
                  pl.BlockSpec((1, 1, cw), lambda l, j: (l, 0, j))],
        out_specs=pl.BlockSpec((1, 8, cw), lambda l, j: (l, 0, j)),
        compiler_params=_cparams(("arbitrary", "arbitrary")),
        name="modulation",
    )(cond, w_mod, b_mod.reshape(DEPTH, 1, 6 * D))


def _prep_kernel(u_ref, v_ref, u16_ref, vt_ref):
    u16_ref[0] = u_ref[0].astype(BF16)
    vt_ref[0] = v_ref[0].T.astype(BF16)


def _prep_experts(peer_u, peer_v):
    eb = 512
    return pl.pallas_call(
        _prep_kernel,
        out_shape=(jax.ShapeDtypeStruct((DEPTH, N_EXPERTS, D), BF16),
                   jax.ShapeDtypeStruct((DEPTH, D, N_EXPERTS), BF16)),
        grid=(DEPTH, N_EXPERTS // eb),
        in_specs=[pl.BlockSpec((1, eb, D), lambda l, j: (l, j, 0)),
                  pl.BlockSpec((1, eb, D), lambda l, j: (l, j, 0))],
        out_specs=(pl.BlockSpec((1, eb, D), lambda l, j: (l, j, 0)),
                   pl.BlockSpec((1, D, eb), lambda l, j: (l, 0, j))),
        compiler_params=_cparams(("arbitrary", "arbitrary")),
        name="prep_experts",
    )(peer_u, peer_v)


def _pre_kernel(x_ref, mod_ref, g_ref, w_ref, qkn_ref, cb_ref, sb_ref, cc_ref, sc_ref,
                aq_ref, ak_ref, av_ref, bq_ref, bk_ref, bv_ref, cq_ref, ck_ref, cv_ref, kv_ref):
    x = x_ref[...]
    mod = mod_ref[0]
    shift1, scale1 = mod[:, 0:D], mod[:, D:2 * D]
    h = _rms(x, g_ref[0]) * (1.0 + scale1) + shift1
    r = jnp.dot(h.astype(BF16), w_ref[0], preferred_element_type=F32)

    o = 0
    aq = r[:, o:o + W_A]; o += W_A
    ak = r[:, o:o + W_A]; o += W_A
    av = r[:, o:o + W_A]; o += W_A
    bq = r[:, o:o + W_BQK]; o += W_BQK
    bk = r[:, o:o + W_BQK]; o += W_BQK
    bv = r[:, o:o + W_BV]; o += W_BV
    cq = r[:, o:o + W_CQ]; o += W_CQ
    ck = r[:, o:o + W_CKV]; o += W_CKV
    cv = r[:, o:o + W_CKV]

    cb, sb, cc, sc = cb_ref[...], sb_ref[...], cc_ref[...], sc_ref[...]
    qkn = qkn_ref[0]

    def rope_b(t):
        return jnp.concatenate([_rope(t[:, s:s + LANES], cb, sb, DQK_B // 4)
                                for s in range(0, t.shape[1], LANES)], axis=1)

    def norm_rope_c(t, gain):
        outs = []
        for s in range(0, t.shape[1], LANES):
            ts = t[:, s:s + LANES]
            ts = ts * lax.rsqrt(_group_mean_sq(ts, HD) + EPS) * gain
            outs.append(_rope(ts, cc, sc, HD // 4))
        return jnp.concatenate(outs, axis=1)

    bq = rope_b(bq)
    bk = rope_b(bk)
    cq = norm_rope_c(cq, qkn[0:1])
    ck = norm_rope_c(ck, qkn[1:2])

    aq_ref[...] = aq.astype(BF16)
    ak_ref[...] = ak.astype(BF16)
    av_ref[...] = av.astype(BF16)
    bq_ref[...] = bq.astype(BF16)
    bk_ref[...] = bk.astype(BF16)
    bv_ref[...] = bv.astype(BF16)
    cq_ref[...] = cq.astype(BF16)
    ck_ref[...] = ck.astype(BF16)
    cv_ref[...] = cv.astype(BF16)
    kv_ref[...] = jnp.concatenate([ak, av, bk, bv, ck, cv], axis=1)


def _pre(l, x, mod, norm_g, w_in16, qkn, tabs):
    widths = (W_A, W_A, W_A, W_BQK, W_BQK, W_BV, W_CQ, W_CKV, W_CKV)
    rope_blk = lambda i: (jnp.where(i < CTX_TILES, 0, 1 + (i - CTX_TILES) % LAT_TILES_PER_BATCH), 0)
    row = lambda i: (i, 0)
    kv_row = lambda i: (jnp.minimum(i, CTX_TILES), 0)
    out_shape = tuple(jax.ShapeDtypeStruct((N_TOK, w), BF16) for w in widths) + (
        jax.ShapeDtypeStruct((N_CTX + TOK_TILE, KV_WIDTH), F32),)
    out_specs = tuple(pl.BlockSpec((TOK_TILE, w), row) for w in widths) + (
        pl.BlockSpec((TOK_TILE, KV_WIDTH), kv_row),)
    return pl.pallas_call(
        _pre_kernel,
        out_shape=out_shape,
        grid=(N_TILES,),
        in_specs=[pl.BlockSpec((TOK_TILE, D), row),
                  pl.BlockSpec((1, 1, 6 * D), lambda i: (_mod_row(i), 0, 0)),
                  pl.BlockSpec((1, 1, D), lambda i: (2 * l, 0, 0)),
                  pl.BlockSpec((1, D, IN_WIDTH), lambda i: (l, 0, 0)),
                  pl.BlockSpec((1, 2, LANES), lambda i: (l, 0, 0)),
                  pl.BlockSpec((TOK_TILE, LANES), rope_blk),
                  pl.BlockSpec((TOK_TILE, LANES), rope_blk),
                  pl.BlockSpec((TOK_TILE, LANES), rope_blk),
                  pl.BlockSpec((TOK_TILE, LANES), rope_blk)],
        out_specs=out_specs,
        compiler_params=_cparams(("arbitrary",)),
        name=f"pre_{l}",
    )(x, mod, norm_g, w_in16, qkn, *tabs)


def _lane_half(shape, half):
    lane = lax.broadcasted_iota(jnp.int32, shape, 1)
    return (lane // HD) == half


def _keep_lanes(x, lo, width):
    lane = lax.broadcasted_iota(jnp.int32, x.shape, 1)
    return jnp.where((lane >= lo) & (lane < lo + width), x, jnp.zeros_like(x))


def _attend(q, kvs, scale, bias=None):
    ss = []
    for i, (k, _) in enumerate(kvs):
        s = lax.dot_general(q, k, _NT, preferred_element_type=F32) * scale
        if i == 0 and bias is not None:
            s = s + bias
        ss.append(s)
    m = jnp.max(ss[0], axis=-1, keepdims=True)
    for s in ss[1:]:
        m = jnp.maximum(m, jnp.max(s, axis=-1, keepdims=True))
    o, den = None, None
    for s, (_, v) in zip(ss, kvs):
        p = jnp.exp(s - m)
        ps = jnp.sum(p, axis=-1, keepdims=True)
        po = jnp.dot(p.astype(BF16), v, preferred_element_type=F32)
        o = po if o is None else o + po
        den = ps if den is None else den + ps
    return o / den


def _diff_lambda(dl_ref, lam_init):
    dl = dl_ref[0]
    a = jnp.sum(dl[0:1] * dl[1:2], axis=-1, keepdims=True)
    b = jnp.sum(dl[2:3] * dl[3:4], axis=-1, keepdims=True)
    return jnp.exp(a) - jnp.exp(b) + lam_init


def _mixer_b(bq, kvs_of_slab, lam, sub, lam_init):
    scale = DQK_B ** -0.5
    slabs = []
    for s in range(W_BV // LANES):
        halves = []
        for half in range(2):
            h = 2 * s + half
            maps = []
            for c in range(2):
                hc = 2 * h + c
                q = _keep_lanes(bq[:, (hc // 4) * LANES:(hc // 4 + 1) * LANES], (hc % 4) * DQK_B, DQK_B)
                maps.append(_attend(q, kvs_of_slab(hc // 4, s), scale))
            halves.append(maps[0] - lam * maps[1])
        od = jnp.where(_lane_half(halves[0].shape, 0), halves[0], halves[1])
        od = od * lax.rsqrt(_group_mean_sq(od, DV_B) + EPS) * sub * (1.0 - lam_init)
        slabs.append(od)
    return jnp.concatenate(slabs, axis=1)


def _mixer_c(cq, kvs):
    slabs = []
    for s in range(W_CQ // LANES):
        qs = cq[:, s * LANES:(s + 1) * LANES]
        o0 = _attend(_keep_lanes(qs, 0, HD), kvs, HD ** -0.5)
        o1 = _attend(_keep_lanes(qs, HD, HD), kvs, HD ** -0.5)
        slabs.append(jnp.where(_lane_half(o0.shape, 0), o0, o1))
    return jnp.concatenate(slabs, axis=1)


def _ctx_attn_kernel(aq_ref, ak_ref, av_ref, bq_ref, bk_ref, bv_ref, cq_ref, ck_ref, cv_ref,
                     dl_ref, sub_ref, o_ref, *, lam_init):
    lam = _diff_lambda(dl_ref, lam_init)
    aq, ak, av = aq_ref[...], ak_ref[...], av_ref[...]
    slabs = []
    for s in range(W_A // LANES):
        sl = slice(s * LANES, (s + 1) * LANES)
        kvs = [(ak[:, sl], av[:, sl])]
        o0 = _attend(_keep_lanes(aq[:, sl], 0, HD), kvs, HD ** -0.5)
        o1 = _attend(_keep_lanes(aq[:, sl], HD, HD), kvs, HD ** -0.5)
        slabs.append(jnp.where(_lane_half(o0.shape, 0), o0, o1))
    oa = jnp.concatenate(slabs, axis=1)

    bk, bv = bk_ref[...], bv_ref[...]
    ob = _mixer_b(bq_ref[...],
                  lambda ks, vs: [(bk[:, ks * LANES:(ks + 1) * LANES], bv[:, vs * LANES:(vs + 1) * LANES])],
                  lam, sub_ref[0], lam_init)
    oc = _mixer_c(cq_ref[...], [(ck_ref[...], cv_ref[...])])
    o_ref[...] = jnp.concatenate([oa, ob, oc], axis=1).astype(BF16)


def _ctx_attention(l, qkv, diff_lambda, sub_tiled):
    widths = (W_A, W_A, W_A, W_BQK, W_BQK, W_BV, W_CQ, W_CKV, W_CKV)
    lam_init = 0.8 - 0.6 * math.exp(-0.3 * l)
    return pl.pallas_call(
        functools.partial(_ctx_attn_kernel, lam_init=lam_init),
        out_shape=jax.ShapeDtypeStruct((N_TOK, D), BF16),
        grid=(BATCH,),
        in_specs=[pl.BlockSpec((SEQ, w), lambda b: (b, 0)) for w in widths] + [
            pl.BlockSpec((1, 4, DQK_B), lambda b: (l, 0, 0)),
            pl.BlockSpec((1, 1, LANES), lambda b: (l, 0, 0))],
        out_specs=pl.BlockSpec((SEQ, D), lambda b: (b, 0)),
        compiler_params=_cparams(("arbitrary",)),
        name=f"ctx_attn_{l}",
    )(*qkv, diff_lambda, sub_tiled)


def _lat_attn_kernel(aq_ref, ak_ref, av_ref, cak_ref, cav_ref, bias0_ref, bias1_ref,
                     bq_ref, bk_ref, bv_ref, cbk_ref, cbv_ref,
                     cq_ref, ck_ref, cv_ref, cck_ref, ccv_ref,
                     dl_ref, sub_ref, o_in_ref, o_ref, *, lam_init):
    del o_in_ref
    qb = pl.program_id(1)
    lam = _diff_lambda(dl_ref, lam_init)

    cak = cak_ref[0, 0].astype(BF16)
    cav = cav_ref[0, 0].astype(BF16)
    rows = []
    for pidx, bias_ref in enumerate((bias0_ref, bias1_ref)):
        j = 2 * qb + pidx
        start = pl.multiple_of(jnp.clip(j - 2, 0, ROWS // 2 - 5) * NA_PAIR, NA_PAIR)
        kwin = ak_ref[pl.ds(start, NA_KEYS), :]
        vwin = av_ref[pl.ds(start, NA_KEYS), :]
        q = aq_ref[pidx * NA_PAIR:(pidx + 1) * NA_PAIR, :]
        slabs = []
        for s in range(W_A // LANES):
            sl = slice(s * LANES, (s + 1) * LANES)
            kvs = [(kwin[:, sl], vwin[:, sl]), (cak[:, sl], cav[:, sl])]
            o0 = _attend(_keep_lanes(q[:, sl], 0, HD), kvs, HD ** -0.5, bias_ref[0, 2 * s])
            o1 = _attend(_keep_lanes(q[:, sl], HD, HD), kvs, HD ** -0.5, bias_ref[0, 2 * s + 1])
            slabs.append(jnp.where(_lane_half(o0.shape, 0), o0, o1))
        rows.append(jnp.concatenate(slabs, axis=1))
    oa = jnp.concatenate(rows, axis=0)

    bk, bv = bk_ref[...], bv_ref[...]
    cbk = cbk_ref[0, 0].astype(BF16)
    cbv = cbv_ref[0, 0].astype(BF16)
    ob = _mixer_b(bq_ref[...],
                  lambda ks, vs: [(bk[:, ks * LANES:(ks + 1) * LANES], bv[:, vs * LANES:(vs + 1) * LANES]),
                                  (cbk[:, ks * LANES:(ks + 1) * LANES], cbv[:, vs * LANES:(vs + 1) * LANES])],
                  lam, sub_ref[0], lam_init)
    oc = _mixer_c(cq_ref[...], [(ck_ref[...], cv_ref[...]),
                                (cck_ref[0, 0].astype(BF16), ccv_ref[0, 0].astype(BF16))])
    o_ref[...] = jnp.concatenate([oa, ob, oc], axis=1).astype(BF16)


def _lat_attention(l, qkv, caches, bias0, bias1, diff_lambda, sub_tiled, o_ctx):
    aq, ak, av, bq, bk, bv, cq, ck, cv = qkv
    cak, cav, cbk, cbv, cck, ccv = caches
    lam_init = 0.8 - 0.6 * math.exp(-0.3 * l)
    nqb = DEC_SEQ // TOK_TILE
    lat0 = N_CTX // TOK_TILE
    qrow = lambda b, q: (lat0 + b * nqb + q, 0)
    krow = lambda b, q: (N_CTX // DEC_SEQ + b, 0)
    cache = lambda b, q: (b, l, 0, 0)
    nb = nqb - 1
    return pl.pallas_call(
        functools.partial(_lat_attn_kernel, lam_init=lam_init),
        out_shape=jax.ShapeDtypeStruct((N_TOK, D), BF16),
        grid=(DEC_BATCH, nqb),
        in_specs=[pl.BlockSpec((TOK_TILE, W_A), qrow),
                  pl.BlockSpec((DEC_SEQ, W_A), krow),
                  pl.BlockSpec((DEC_SEQ, W_A), krow),
                  pl.BlockSpec((1, 1, PAST, W_A), cache),
                  pl.BlockSpec((1, 1, PAST, W_A), cache),
                  pl.BlockSpec((1, H_A, NA_PAIR, NA_KEYS),
                               lambda b, q: (l * 3 + jnp.where(q == 0, 0, jnp.where(q == nb, 2, 1)), 0, 0, 0)),
                  pl.BlockSpec((1, H_A, NA_PAIR, NA_KEYS),
                               lambda b, q: (l * 3 + jnp.where(q == 0, 0, jnp.where(q == nb, 2, 1)), 0, 0, 0)),
                  pl.BlockSpec((TOK_TILE, W_BQK), qrow),
                  pl.BlockSpec((DEC_SEQ, W_BQK), krow),
                  pl.BlockSpec((DEC_SEQ, W_BV), krow),
                  pl.BlockSpec((1, 1, PAST, W_BQK), cache),
                  pl.BlockSpec((1, 1, PAST, W_BV), cache),
                  pl.BlockSpec((TOK_TILE, W_CQ), qrow),
                  pl.BlockSpec((DEC_SEQ, W_CKV), krow),
                  pl.BlockSpec((DEC_SEQ, W_CKV), krow),
                  pl.BlockSpec((1, 1, PAST, W_CKV), cache),
                  pl.BlockSpec((1, 1, PAST, W_CKV), cache),
                  pl.BlockSpec((1, 4, DQK_B), lambda b, q: (l, 0, 0)),
                  pl.BlockSpec((1, 1, LANES), lambda b, q: (l, 0, 0)),
                  pl.BlockSpec(memory_space=pl.ANY)],
        out_specs=pl.BlockSpec((TOK_TILE, D), qrow),
        input_output_aliases={19: 0},
        compiler_params=_cparams(("arbitrary", "arbitrary")),
        name=f"lat_attn_{l}",
    )(aq, ak, av, cak, cav, bias0, bias1, bq, bk, bv, cbk, cbv, cq, ck, cv, cck, ccv,
      diff_lambda, sub_tiled, o_ctx)


def _post_kernel(o_ref, x_ref, mod_ref, g_ref, w_ref, x1_ref, h2t_ref):
    mod = mod_ref[0]
    gate1, shift2, scale2 = mod[:, 2 * D:3 * D], mod[:, 3 * D:4 * D], mod[:, 4 * D:5 * D]
    x1 = x_ref[...] + gate1 * jnp.dot(o_ref[...], w_ref[0], preferred_element_type=F32)
    x1_ref[...] = x1
    h2 = _rms(x1, g_ref[0]) * (1.0 + scale2) + shift2
    h2t_ref[...] = h2.T.astype(BF16)


def _post(l, o16, x, mod, norm_g, w_out16):
    row = lambda i: (i, 0)
    return pl.pallas_call(
        _post_kernel,
        out_shape=(jax.ShapeDtypeStruct((N_TOK, D), F32),
                   jax.ShapeDtypeStruct((D, N_TOK), BF16)),
        grid=(N_TILES,),
        in_specs=[pl.BlockSpec((TOK_TILE, D), row),
                  pl.BlockSpec((TOK_TILE, D), row),
                  pl.BlockSpec((1, 1, 6 * D), lambda i: (_mod_row(i), 0, 0)),
                  pl.BlockSpec((1, 1, D), lambda i: (2 * l + 1, 0, 0)),
                  pl.BlockSpec((1, D, D), lambda i: (l, 0, 0))],
        out_specs=(pl.BlockSpec((TOK_TILE, D), row),
                   pl.BlockSpec((D, TOK_TILE), lambda i: (0, i))),
        compiler_params=_cparams(("arbitrary",)),
        name=f"post_{l}",
    )(o16, x, mod, norm_g, w_out16)


def _top16(x):
    kio = lax.broadcasted_iota(jnp.int32, x.shape, 0).astype(F32)
    rank = jnp.full(x.shape, 99.0, F32)
    vals = []
    for r in range(PEER_TOPK):
        m = jnp.max(x, axis=0, keepdims=True)
        kmin = jnp.min(jnp.where(x == m, kio, float(N_KEYS)), axis=0, keepdims=True)
        sel = kio == kmin
        rank = jnp.where(sel, float(r), rank)
        x = jnp.where(sel, -jnp.inf, x)
        vals.append(m)
    return rank, jnp.concatenate(vals, axis=0)


def _merge16(pairs):
    io = lax.broadcasted_iota(jnp.int32, pairs[0][0].shape, 0).astype(F32)
    cnt = [jnp.zeros(s1r.shape, F32) for s1r, _ in pairs]
    head = [s1r + s2r[0:1] for s1r, s2r in pairs]
    top = [hd[0:1] for hd in head]
    z = [jnp.zeros_like(t) for t in top]
    for _ in range(PEER_TOPK):
        for i, (s1r, s2r) in enumerate(pairs):
            m = jnp.max(head[i], axis=0, keepdims=True)
            z[i] = z[i] + jnp.exp(m - top[i])
            row = jnp.min(jnp.where(head[i] == m, io, 99.0), axis=0, keepdims=True)
            hit = io == row
            cnt[i] = cnt[i] + jnp.where(hit, 1.0, 0.0)
            nxt_idx = jnp.max(jnp.where(hit, cnt[i], 0.0), axis=0, keepdims=True)
            nxt = jnp.max(jnp.where(io == nxt_idx, s2r, -jnp.inf), axis=0, keepdims=True)
            head[i] = jnp.where(hit, s1r + nxt, head[i])
    return list(zip(cnt, z))


def _topk_kernel(h2t_ref, wqt_ref, keys_ref, rank2_ref, e2_ref, lk_ref, cw_ref, st_ref, sr_ref):
    qt = jnp.dot(wqt_ref[0], h2t_ref[...], preferred_element_type=F32).astype(BF16)
    for hc in range(2 * PEER_HEADS):
        st_ref[hc] = jnp.dot(keys_ref[0, hc], qt[hc * N_KEYS:(hc + 1) * N_KEYS, :],
                             preferred_element_type=F32)

    def head_body(h, carry):
        s1 = st_ref[2 * h]
        s2 = st_ref[2 * h + 1]
        rank1, s1r = _top16(s1)
        rank2, s2r = _top16(s2)
        sr_ref[2 * h] = s1r
        sr_ref[2 * h + 1] = s2r
        rank2_ref[h] = rank2.astype(BF16)
        e2_ref[h] = jnp.exp(s2 - s2r[0:1]).astype(BF16)
        lk_ref[h] = rank1
        cw_ref[h] = jnp.exp(s1 - s1r[0:1])
        return carry

    lax.fori_loop(0, PEER_HEADS, head_body, 0)

    merged = _merge16([(sr_ref[2 * h], sr_ref[2 * h + 1]) for h in range(PEER_HEADS)])
    for h, (cnt, z) in enumerate(merged):
        rank1 = lk_ref[h]
        lk = jnp.zeros_like(rank1)
        for r in range(PEER_TOPK):
            lk = jnp.where(rank1 == float(r), cnt[r:r + 1], lk)
        lk_ref[h] = lk
        cw_ref[h] = cw_ref[h] / z


def _topk(l, h2t, wqt16, keys16):
    blk = pl.BlockSpec((PEER_HEADS, N_KEYS, TOPK_CHUNK), lambda i: (0, 0, i))
    shp = jax.ShapeDtypeStruct((PEER_HEADS, N_KEYS, N_TOK), F32)
    shp16 = jax.ShapeDtypeStruct((PEER_HEADS, N_KEYS, N_TOK), BF16)
    return pl.pallas_call(
        _topk_kernel,
        out_shape=(shp16, shp16, shp, shp),
        grid=(N_TOK // TOPK_CHUNK,),
        in_specs=[pl.BlockSpec((D, TOPK_CHUNK), lambda i: (0, i)),
                  pl.BlockSpec((1, 2 * PEER_HEADS * N_KEYS, D), lambda i: (l, 0, 0)),
                  pl.BlockSpec((1, 2 * PEER_HEADS, N_KEYS, N_KEYS), lambda i: (l, 0, 0, 0))],
        out_specs=(blk, blk, blk, blk),
        scratch_shapes=[pltpu.VMEM((2 * PEER_HEADS, N_KEYS, TOPK_CHUNK), F32),
                        pltpu.VMEM((2 * PEER_HEADS, PEER_TOPK, TOPK_CHUNK), F32)],
        compiler_params=_cparams(("arbitrary",)),
        name=f"topk_{l}",
    )(h2t, wqt16, keys16)


def _gelu(x):
    c = math.sqrt(2.0 / math.pi)
    return 0.5 * x * (1.0 + jnp.tanh(c * (x + 0.044715 * (x * x * x))))


def _expert_kernel(h2t_ref, u_ref, vt_ref, rank2_ref, e2_ref, lk_ref, cw_ref, x1_ref, mod_ref,
                   o_ref, acc_ref, at0_ref, at1_ref):
    j = pl.program_id(1)
    nblk = pl.num_programs(1) - 1

    @pl.when(j == 0)
    def _():
        acc_ref[...] = jnp.zeros_like(acc_ref)
        at1_ref[...] = jnp.zeros_like(at1_ref)

    def step(at_w, at_r):
        at_w[...] = jnp.dot(u_ref[0], h2t_ref[...], preferred_element_type=F32)
        part = None
        per = EXP_CHUNK // N_KEYS
        for c in range(EXP_BLK // EXP_CHUNK):
            ags = []
            for a in range(c * per, (c + 1) * per):
                g = None
                for h in range(PEER_HEADS):
                    lk = lk_ref[h, a:a + 1, :].astype(BF16)
                    cw = cw_ref[h, a:a + 1, :].astype(BF16)
                    contrib = jnp.where(rank2_ref[h] < lk, e2_ref[h] * cw, jnp.zeros((), BF16))
                    g = contrib if g is None else g + contrib
                ags.append(_gelu(at_r[a * N_KEYS:(a + 1) * N_KEYS, :]).astype(BF16) * g)
            d = jnp.dot(vt_ref[0, :, c * EXP_CHUNK:(c + 1) * EXP_CHUNK], jnp.concatenate(ags, axis=0),
                        preferred_element_type=F32)
            part = d if part is None else part + d
        acc_ref[...] += part

    @pl.when(j % 2 == 0)
    def _():
        step(at0_ref, at1_ref)

    @pl.when(j % 2 == 1)
    def _():
        step(at1_ref, at0_ref)

    @pl.when(j == nblk)
    def _():
        gate2 = mod_ref[0][:, 5 * D:6 * D]
        o_ref[...] = x1_ref[...] + gate2 * acc_ref[...].T


def _experts(l, h2t, u16, vt16, rank2, e2, lk, cw, x1, mod):
    nblk = N_EXPERTS // EXP_BLK
    tok = pl.BlockSpec((PEER_HEADS, N_KEYS, EXP_TOK), lambda i, j: (0, 0, i))
    sel = pl.BlockSpec((PEER_HEADS, EXP_BLK // N_KEYS, EXP_TOK), lambda i, j: (0, jnp.maximum(j - 1, 0), i))
    return pl.pallas_call(
        _expert_kernel,
        out_shape=jax.ShapeDtypeStruct((N_TOK, D), F32),
        grid=(N_TOK // EXP_TOK, nblk + 1),
        in_specs=[pl.BlockSpec((D, EXP_TOK), lambda i, j: (0, i)),
                  pl.BlockSpec((1, EXP_BLK, D), lambda i, j: (l, jnp.minimum(j, nblk - 1), 0)),
                  pl.BlockSpec((1, D, EXP_BLK), lambda i, j: (l, 0, jnp.maximum(j - 1, 0))),
                  tok, tok, sel, sel,
                  pl.BlockSpec((EXP_TOK, D), lambda i, j: (i, 0)),
                  pl.BlockSpec((1, 1, 6 * D), lambda i, j: (_mod_row(i * (EXP_TOK // TOK_TILE)), 0, 0))],
        out_specs=pl.BlockSpec((EXP_TOK, D), lambda i, j: (i, 0)),
        scratch_shapes=[pltpu.VMEM((D, EXP_TOK), F32),
                        pltpu.VMEM((EXP_BLK, EXP_TOK), F32),
                        pltpu.VMEM((EXP_BLK, EXP_TOK), F32)],
        compiler_params=_cparams(("arbitrary", "arbitrary")),
        name=f"experts_{l}",
    )(h2t, u16, vt16, rank2, e2, lk, cw, x1, mod)


def _final_kernel(x_ref, g_ref, o_ref):
    o_ref[...] = _rms(x_ref[...], g_ref[...])


def _final_norm(x, g):
    row = lambda i: (i, 0)
    return pl.pallas_call(
        _final_kernel,
        out_shape=jax.ShapeDtypeStruct((N_TOK, D), F32),
        grid=(N_TILES,),
        in_specs=[pl.BlockSpec((TOK_TILE, D), row), pl.BlockSpec((1, D), lambda i: (0, 0))],
        out_specs=pl.BlockSpec((TOK_TILE, D), row),
        compiler_params=_cparams(("arbitrary",)),
        name="final_norm",
    )(x, g.reshape(1, D))


def _rope_tables():
    t = jnp.arange(DEC_SEQ)
    rowp = (t // GRID_W).astype(F32)[:, None]
    colp = (t % GRID_W).astype(F32)[:, None]
    lane = np.arange(LANES)

    def tab(d):
        q = d // 4
        dd = lane % d
        inv = ROPE_THETA ** (-jnp.arange(q, dtype=F32) / q)
        ang = jnp.where((dd < d // 2)[None, :], rowp, colp) * inv[dd % q][None, :]
        sign = np.where((dd % (d // 2)) < q, -1.0, 1.0).astype(np.float32)[None, :]
        cos = jnp.concatenate([jnp.ones((TOK_TILE, LANES), F32), jnp.cos(ang)], axis=0)
        sin = jnp.concatenate([jnp.zeros((TOK_TILE, LANES), F32), jnp.sin(ang) * sign], axis=0)
        return cos, sin

    cb, sb = tab(DQK_B)
    cc, sc = tab(HD)
    return cb, sb, cc, sc


def _na_bias_index():
    def one(j):
        m0 = min(max(j - 2, 0), ROWS // 2 - 5)
        r = 2 * j + np.arange(2)
        rs = np.clip(r - NA_ROWS // 2, 0, ROWS - NA_ROWS)
        krow = 2 * m0 + np.arange(NA_KEYS // GRID_W)
        row_ok = (krow[None, :] >= rs[:, None]) & (krow[None, :] < rs[:, None] + NA_ROWS)
        dr = np.clip(krow[None, :] - r[:, None] + NA_ROWS - 1, 0, 2 * NA_ROWS - 2)
        qc = np.arange(GRID_W)
        cs = np.clip(qc - NA_COLS // 2, 0, GRID_W - NA_COLS)
        col_ok = (qc[None, :] >= cs[:, None]) & (qc[None, :] < cs[:, None] + NA_COLS)
        ok = row_ok[:, None, :, None] & col_ok[None, :, None, :]
        return dr, ok.reshape(NA_PAIR, NA_KEYS)

    mid = ROWS // 4
    first = [one(j) for j in (0, mid, ROWS // 2 - 2)]
    second = [one(j) for j in (1, mid, ROWS // 2 - 1)]
    stack = lambda xs, k: np.stack([x[k] for x in xs])
    return (stack(first, 0), stack(first, 1)), (stack(second, 0), stack(second, 1))


def _na_bias(na_rpb, dr, ok):
    qc = np.arange(GRID_W)
    dc = np.clip(qc[None, :] - qc[:, None], -(NA_COLS - 1), NA_COLS - 1) + NA_COLS - 1
    onehot = (dc[None] == np.arange(2 * NA_COLS - 1)[:, None, None]).astype(np.float32)
    rows = jnp.einsum('lhrd,dn->lhrn', na_rpb, onehot.reshape(2 * NA_COLS - 1, GRID_W * GRID_W),
                      precision=lax.Precision.HIGHEST)
    nkr = NA_KEYS // GRID_W
    t = rows[:, :, dr].reshape(DEPTH, H_A, 3, 2, nkr, GRID_W, GRID_W)
    t = t.transpose(0, 2, 1, 3, 5, 4, 6).reshape(DEPTH, 3, H_A, NA_PAIR, NA_KEYS)
    return jnp.where(ok[None, :, None], t, NEG).reshape(DEPTH * 3, H_A, NA_PAIR, NA_KEYS)


def kernel(x_prompt, x_sample, cache_a_k, cache_a_v, cache_b_k, cache_b_v, cache_c_k, cache_c_v,
           c, c_ctx, w_mod, b_mod, norm_g, w_in, w_out, na_rpb, diff_lambda, diff_subln,
           gqa_qk_norm, peer_wq, peer_keys, peer_u, peer_v, final_g):
    cq0 = 3 * W_A + 2 * W_BQK + W_BV
    g = H_C // KV_C
    w_cq = w_in[:, :, cq0:cq0 + W_CQ].reshape(DEPTH, D, KV_C, g, HD).transpose(0, 1, 3, 2, 4)
    w_in16 = jnp.concatenate([w_in[:, :, :cq0], w_cq.reshape(DEPTH, D, W_CQ), w_in[:, :, cq0 + W_CQ:]],
                             axis=2).astype(BF16)
    oc0 = W_A + W_BV
    w_oc = w_out[:, oc0:].reshape(DEPTH, KV_C, g, HD, D).transpose(0, 2, 1, 3, 4)
    w_out16 = jnp.concatenate([w_out[:, :oc0], w_oc.reshape(DEPTH, W_CQ, D)], axis=1).astype(BF16)
    wqt16 = peer_wq.transpose(0, 2, 1).astype(BF16)
    keys16 = peer_keys.reshape(DEPTH, 2 * PEER_HEADS, N_KEYS, N_KEYS).astype(BF16)
    u16, vt16 = _prep_experts(peer_u, peer_v)

    cond = jnp.concatenate([c_ctx[None, :], c, jnp.zeros((3, D), F32)], axis=0)
    mod_all = _modulation(cond, w_mod, b_mod)[:, :1 + DEC_BATCH].reshape(DEPTH, 1 + DEC_BATCH, 1, 6 * D)
    norm_g3 = norm_g.reshape(2 * DEPTH, 1, D)
    qkn = jnp.tile(gqa_qk_norm, (1, 1, LANES // HD))
    sub_tiled = jnp.tile(diff_subln, (1, LANES // DV_B)).reshape(DEPTH, 1, LANES)
    tabs = _rope_tables()
    (dr0, ok0), (dr1, ok1) = _na_bias_index()
    bias0 = _na_bias(na_rpb, dr0, ok0)
    bias1 = _na_bias(na_rpb, dr1, ok1)
    caches = (cache_a_k.reshape(DEC_BATCH, DEPTH, PAST, W_A), cache_a_v.reshape(DEC_BATCH, DEPTH, PAST, W_A),
              cache_b_k.reshape(DEC_BATCH, DEPTH, PAST, W_BQK), cache_b_v.reshape(DEC_BATCH, DEPTH, PAST, W_BV),
              cache_c_k.reshape(DEC_BATCH, DEPTH, PAST, W_CKV), cache_c_v.reshape(DEC_BATCH, DEPTH, PAST, W_CKV))

    x = jnp.concatenate([x_prompt.reshape(N_CTX, D), x_sample.reshape(N_LAT, D)], axis=0)
    new_kv = []
    for l in range(DEPTH):
        mod = mod_all[l]
        *qkv, kv = _pre(l, x, mod, norm_g3, w_in16, qkn, tabs)
        new_kv.append(kv[:N_CTX].reshape(BATCH, SEQ, KV_WIDTH))
        o16 = _ctx_attention(l, qkv, diff_lambda, sub_tiled)
        o16 = _lat_attention(l, qkv, caches, bias0, bias1, diff_lambda, sub_tiled, o16)
        x1, h2t = _post(l, o16, x, mod, norm_g3, w_out16)
        rank2, e2, lk, cw = _topk(l, h2t, wqt16, keys16)
        x = _experts(l, h2t, u16, vt16, rank2, e2, lk, cw, x1, mod)
    y = _final_norm(x, final_g)

    kv = jnp.stack(new_kv, axis=1)
    o = 0
    parts = []
    for w, shp in ((W_A, (H_A, HD)), (W_A, (H_A, HD)), (W_BQK, (H_B, 2, DQK_B)),
                   (W_BV, (H_B, DV_B)), (W_CKV, (KV_C, HD)), (W_CKV, (KV_C, HD))):
        parts.append(kv[..., o:o + w].reshape(BATCH, DEPTH, SEQ, *shp))
        o += w
    return (y[:N_CTX].reshape(BATCH, SEQ, D), y[N_CTX:].reshape(DEC_BATCH, DEC_SEQ, D), *parts)
```

```python
import functools
import math

import numpy as np
import jax
import jax.numpy as jnp
from jax import lax
from jax.experimental import pallas as pl
from jax.experimental.pallas import tpu as pltpu

F32 = jnp.float32
BF16 = jnp.bfloat16

D = 1024
BATCH, SEQ = 16, 256
DEPTH = 4
DEC_BATCH, DEC_SEQ = 4, 2048
PAST = 256
GRID_W = 64
ROWS = DEC_SEQ // GRID_W
HD = 64
H_A, H_B, H_C, KV_C = 6, 4, 6, 2
DQK_B, DV_B = 32, 64
NA_ROWS, NA_COLS = 8, 16
W_A = H_A * HD
W_BQK = H_B * 2 * DQK_B
W_BV = H_B * DV_B
W_CQ = H_C * HD
W_CKV = KV_C * HD
IN_WIDTH = 3 * W_A + 2 * W_BQK + W_BV + W_CQ + 2 * W_CKV
KV_WIDTH = 2 * W_A + W_BQK + W_BV + 2 * W_CKV
N_KEYS = 128
N_EXPERTS = N_KEYS * N_KEYS
PEER_HEADS, PEER_TOPK = 8, 16
ROPE_THETA = 10000.0
EPS = 1e-6

N_CTX = BATCH * SEQ
N_LAT = DEC_BATCH * DEC_SEQ
N_TOK = N_CTX + N_LAT

LANES = 128
TOK_TILE = 256
N_TILES = N_TOK // TOK_TILE
CTX_TILES = N_CTX // TOK_TILE
LAT_TILES_PER_BATCH = DEC_SEQ // TOK_TILE
NA_PAIR = 2 * GRID_W
NA_KEYS = 5 * NA_PAIR
TOPK_CHUNK = 128
EXP_TOK = 512
EXP_BLK = 1024
EXP_CHUNK = 256
NEG = -1e30
VMEM_LIMIT = 56 * 1024 * 1024

_NT = (((1,), (1,)), ((), ()))


def _cparams(sem):
    return pltpu.CompilerParams(dimension_semantics=sem, vmem_limit_bytes=VMEM_LIMIT)


def _mod_row(i):
    return jnp.where(i < CTX_TILES, 0, 1 + (i - CTX_TILES) // LAT_TILES_PER_BATCH)


def _rms(x, g):
    return x * lax.rsqrt(jnp.mean(x * x, axis=-1, keepdims=True) + EPS) * g


def _group_mean_sq(x, width):
    r = lax.broadcasted_iota(jnp.int32, (LANES, LANES), 0) // width
    c = lax.broadcasted_iota(jnp.int32, (LANES, LANES), 1) // width
    bd = jnp.where(r == c, 1.0 / width, 0.0).astype(BF16)
    t = x * x
    hi = t.astype(BF16)
    lo = (t - hi.astype(F32)).astype(BF16)
    return (jnp.dot(hi, bd, preferred_element_type=F32)
            + jnp.dot(lo, bd, preferred_element_type=F32))


def _rope(x, cos, sin_signed, half):
    lane = lax.broadcasted_iota(jnp.int32, x.shape, 1)
    first = (lane % (2 * half)) < half
    rot = jnp.where(first, pltpu.roll(x, LANES - half, 1), pltpu.roll(x, half, 1))
    return x * cos + rot * sin_signed


def _mod_kernel(cond_ref, w_ref, b_ref, o_ref):
    c = cond_ref[...]
    s = c / (1.0 + jnp.exp(-c))
    o_ref[0] = jnp.dot(s.astype(BF16), w_ref[0].astype(BF16),
                       preferred_element_type=F32) + b_ref[0]


def _modulation(cond, w_mod, b_mod):
    ncol = 4
    cw = 6 * D // ncol
    return pl.pallas_call(
        _mod_kernel,
        out_shape=jax.ShapeDtypeStruct((DEPTH, 8, 6 * D), F32),
        grid=(DEPTH, ncol),
        in_specs=[pl.BlockSpec((8, D), lambda l, j: (0, 0)),
                  pl.BlockSpec((1, D, cw), lambda l, j: (l, 0, j)),
                  pl.BlockSpec((1, 1, cw), lambda l, j: (l, 0, j))],
        out_specs=pl.BlockSpec((1, 8, cw), lambda l, j: (l, 0, j)),
        compiler_params=_cparams(("arbitrary", "arbitrary")),
        name="modulation",
    )(cond, w_mod, b_mod.reshape(DEPTH, 1, 6 * D))


def _prep_kernel(u_ref, v_ref, u16_ref, vt_ref):
    u16_ref[0] = u_ref[0].astype(BF16)
    vt_ref[0] = v_ref[0].T.astype(BF16)


def _prep_experts(peer_u, peer_v):
    eb = 512
    return pl.pallas_call(
        _prep_kernel,
        out_shape=(jax.ShapeDtypeStruct((DEPTH, N_EXPERTS, D), BF16),
                   jax.ShapeDtypeStruct((DEPTH, D, N_EXPERTS), BF16)),
        grid=(DEPTH, N_EXPERTS // eb),
        in_specs=[pl.BlockSpec((1, eb, D), lambda l, j: (l, j, 0)),
                  pl.BlockSpec((1, eb, D), lambda l, j: (l, j, 0))],
        out_specs=(pl.BlockSpec((1, eb, D), lambda l, j: (l, j, 0)),
                   pl.BlockSpec((1, D, eb), lambda l, j: (l, 0, j))),
        compiler_params=_cparams(("arbitrary", "arbitrary")),
        name="prep_experts",
    )(peer_u, peer_v)


def _pre_kernel(x_ref, mod_ref, g_ref, w_ref, qkn_ref, cb_ref, sb_ref, cc_ref, sc_ref,
                aq_ref, ak_ref, av_ref, bq_ref, bk_ref, bv_ref, cq_ref, ck_ref, cv_ref, kv_ref):
    x = x_ref[...]
    mod = mod_ref[0]
    shift1, scale1 = mod[:, 0:D], mod[:, D:2 * D]
    h = _rms(x, g_ref[0]) * (1.0 + scale1) + shift1
    r = jnp.dot(h.astype(BF16), w_ref[0], preferred_element_type=F32)

    o = 0
    aq = r[:, o:o + W_A]; o += W_A
    ak = r[:, o:o + W_A]; o += W_A
    av = r[:, o:o + W_A]; o += W_A
    bq = r[:, o:o + W_BQK]; o += W_BQK
    bk = r[:, o:o + W_BQK]; o += W_BQK
    bv = r[:, o:o + W_BV]; o += W_BV
    cq = r[:, o:o + W_CQ]; o += W_CQ
    ck = r[:, o:o + W_CKV]; o += W_CKV
    cv = r[:, o:o + W_CKV]

    cb, sb, cc, sc = cb_ref[...], sb_ref[...], cc_ref[...], sc_ref[...]
    qkn = qkn_ref[0]

    def rope_b(t):
        return jnp.concatenate([_rope(t[:, s:s + LANES], cb, sb, DQK_B // 4)
                                for s in range(0, t.shape[1], LANES)], axis=1)

    def norm_rope_c(t, gain):
        outs = []
        for s in range(0, t.shape[1], LANES):
            ts = t[:, s:s + LANES]
            ts = ts * lax.rsqrt(_group_mean_sq(ts, HD) + EPS) * gain
            outs.append(_rope(ts, cc, sc, HD // 4))
        return jnp.concatenate(outs, axis=1)

    bq = rope_b(bq)
    bk = rope_b(bk)
    cq = norm_rope_c(cq, qkn[0:1])
    ck = norm_rope_c(ck, qkn[1:2])

    aq_ref[...] = aq.astype(BF16)
    ak_ref[...] = ak.astype(BF16)
    av_ref[...] = av.astype(BF16)
    bq_ref[...] = bq.astype(BF16)
    bk_ref[...] = bk.astype(BF16)
    bv_ref[...] = bv.astype(BF16)
    cq_ref[...] = cq.astype(BF16)
    ck_ref[...] = ck.astype(BF16)
    cv_ref[...] = cv.astype(BF16)
    kv_ref[...] = jnp.concatenate([ak, av, bk, bv, ck, cv], axis=1)


def _pre(l, x, mod, norm_g, w_in16, qkn, tabs):
    widths = (W_A, W_A, W_A, W_BQK, W_BQK, W_BV, W_CQ, W_CKV, W_CKV)
    rope_blk = lambda i: (jnp.where(i < CTX_TILES, 0, 1 + (i - CTX_TILES) % LAT_TILES_PER_BATCH), 0)
    row = lambda i: (i, 0)
    kv_row = lambda i: (jnp.minimum(i, CTX_TILES), 0)
    out_shape = tuple(jax.ShapeDtypeStruct((N_TOK, w), BF16) for w in widths) + (
        jax.ShapeDtypeStruct((N_CTX + TOK_TILE, KV_WIDTH), F32),)
    out_specs = tuple(pl.BlockSpec((TOK_TILE, w), row) for w in widths) + (
        pl.BlockSpec((TOK_TILE, KV_WIDTH), kv_row),)
    return pl.pallas_call(
        _pre_kernel,
        out_shape=out_shape,
        grid=(N_TILES,),
        in_specs=[pl.BlockSpec((TOK_TILE, D), row),
                  pl.BlockSpec((1, 1, 6 * D), lambda i: (_mod_row(i), 0, 0)),
                  pl.BlockSpec((1, 1, D), lambda i: (2 * l, 0, 0)),
                  pl.BlockSpec((1, D, IN_WIDTH), lambda i: (l, 0, 0)),
                  pl.BlockSpec((1, 2, LANES), lambda i: (l, 0, 0)),
                  pl.BlockSpec((TOK_TILE, LANES), rope_blk),
                  pl.BlockSpec((TOK_TILE, LANES), rope_blk),
                  pl.BlockSpec((TOK_TILE, LANES), rope_blk),
                  pl.BlockSpec((TOK_TILE, LANES), rope_blk)],
        out_specs=out_specs,
        compiler_params=_cparams(("arbitrary",)),
        name=f"pre_{l}",
    )(x, mod, norm_g, w_in16, qkn, *tabs)


def _lane_half(shape, half):
    lane = lax.broadcasted_iota(jnp.int32, shape, 1)
    return (lane // HD) == half


def _keep_lanes(x, lo, width):
    lane = lax.broadcasted_iota(jnp.int32, x.shape, 1)
    return jnp.where((lane >= lo) & (lane < lo + width), x, jnp.zeros_like(x))


def _attend(q, kvs, scale, bias=None):
    ss = []
    for i, (k, _) in enumerate(kvs):
        s = lax.dot_general(q, k, _NT, preferred_element_type=F32) * scale
        if i == 0 and bias is not None:
            s = s + bias
        ss.append(s)
    m = jnp.max(ss[0], axis=-1, keepdims=True)
    for s in ss[1:]:
        m = jnp.maximum(m, jnp.max(s, axis=-1, keepdims=True))
    o, den = None, None
    for s, (_, v) in zip(ss, kvs):
        p = jnp.exp(s - m)
        ps = jnp.sum(p, axis=-1, keepdims=True)
        po = jnp.dot(p.astype(BF16), v, preferred_element_type=F32)
        o = po if o is None else o + po
        den = ps if den is None else den + ps
    return o / den


def _diff_lambda(dl_ref, lam_init):
    dl = dl_ref[0]
    a = jnp.sum(dl[0:1] * dl[1:2], axis=-1, keepdims=True)
    b = jnp.sum(dl[2:3] * dl[3:4], axis=-1, keepdims=True)
    return jnp.exp(a) - jnp.exp(b) + lam_init


def _mixer_b(bq, kvs_of_slab, lam, sub, lam_init):
    scale = DQK_B ** -0.5
    slabs = []
    for s in range(W_BV // LANES):
        halves = []
        for half in range(2):
            h = 2 * s + half
            maps = []
            for c in range(2):
                hc = 2 * h + c
                q = _keep_lanes(bq[:, (hc // 4) * LANES:(hc // 4 + 1) * LANES], (hc % 4) * DQK_B, DQK_B)
                maps.append(_attend(q, kvs_of_slab(hc // 4, s), scale))
            halves.append(maps[0] - lam * maps[1])
        od = jnp.where(_lane_half(halves[0].shape, 0), halves[0], halves[1])
        od = od * lax.rsqrt(_group_mean_sq(od, DV_B) + EPS) * sub * (1.0 - lam_init)
        slabs.append(od)
    return jnp.concatenate(slabs, axis=1)


def _mixer_c(cq, kvs):
    slabs = []
    for s in range(W_CQ // LANES):
        qs = cq[:, s * LANES:(s + 1) * LANES]
        o0 = _attend(_keep_lanes(qs, 0, HD), kvs, HD ** -0.5)
        o1 = _attend(_keep_lanes(qs, HD, HD), kvs, HD ** -0.5)
        slabs.append(jnp.where(_lane_half(o0.shape, 0), o0, o1))
    return jnp.concatenate(slabs, axis=1)


def _ctx_attn_kernel(aq_ref, ak_ref, av_ref, bq_ref, bk_ref, bv_ref, cq_ref, ck_ref, cv_ref,
                     dl_ref, sub_ref, o_ref, *, lam_init):
    lam = _diff_lambda(dl_ref, lam_init)
    aq, ak, av = aq_ref[...], ak_ref[...], av_ref[...]
    slabs = []
    for s in range(W_A // LANES):
        sl = slice(s * LANES, (s + 1) * LANES)
        kvs = [(ak[:, sl], av[:, sl])]
        o0 = _attend(_keep_lanes(aq[:, sl], 0, HD), kvs, HD ** -0.5)
        o1 = _attend(_keep_lanes(aq[:, sl], HD, HD), kvs, HD ** -0.5)
        slabs.append(jnp.where(_lane_half(o0.shape, 0), o0, o1))
    oa = jnp.concatenate(slabs, axis=1)

    bk, bv = bk_ref[...], bv_ref[...]
    ob = _mixer_b(bq_ref[...],
                  lambda ks, vs: [(bk[:, ks * LANES:(ks + 1) * LANES], bv[:, vs * LANES:(vs + 1) * LANES])],
                  lam, sub_ref[0], lam_init)
    oc = _mixer_c(cq_ref[...], [(ck_ref[...], cv_ref[...])])
    o_ref[...] = jnp.concatenate([oa, ob, oc], axis=1).astype(BF16)


def _ctx_attention(l, qkv, diff_lambda, sub_tiled):
    widths = (W_A, W_A, W_A, W_BQK, W_BQK, W_BV, W_CQ, W_CKV, W_CKV)
    lam_init = 0.8 - 0.6 * math.exp(-0.3 * l)
    return pl.pallas_call(
        functools.partial(_ctx_attn_kernel, lam_init=lam_init),
        out_shape=jax.ShapeDtypeStruct((N_CTX, D), BF16),
        grid=(BATCH,),
        in_specs=[pl.BlockSpec((SEQ, w), lambda b: (b, 0)) for w in widths] + [
            pl.BlockSpec((1, 4, DQK_B), lambda b: (l, 0, 0)),
            pl.BlockSpec((1, 1, LANES), lambda b: (l, 0, 0))],
        out_specs=pl.BlockSpec((SEQ, D), lambda b: (b, 0)),
        compiler_params=_cparams(("arbitrary",)),
        name=f"ctx_attn_{l}",
    )(*qkv, diff_lambda, sub_tiled)


def _lat_attn_kernel(aq_ref, ak_ref, av_ref, cak_ref, cav_ref, bias0_ref, bias1_ref,
                     bq_ref, bk_ref, bv_ref, cbk_ref, cbv_ref,
                     cq_ref, ck_ref, cv_ref, cck_ref, ccv_ref,
                     dl_ref, sub_ref, o_ref, *, lam_init):
    qb = pl.program_id(1)
    lam = _diff_lambda(dl_ref, lam_init)

    cak = cak_ref[0, 0].astype(BF16)
    cav = cav_ref[0, 0].astype(BF16)
    rows = []
    for pidx, bias_ref in enumerate((bias0_ref, bias1_ref)):
        j = 2 * qb + pidx
        start = pl.multiple_of(jnp.clip(j - 2, 0, ROWS // 2 - 5) * NA_PAIR, NA_PAIR)
        kwin = ak_ref[pl.ds(start, NA_KEYS), :]
        vwin = av_ref[pl.ds(start, NA_KEYS), :]
        q = aq_ref[pidx * NA_PAIR:(pidx + 1) * NA_PAIR, :]
        slabs = []
        for s in range(W_A // LANES):
            sl = slice(s * LANES, (s + 1) * LANES)
            kvs = [(kwin[:, sl], vwin[:, sl]), (cak[:, sl], cav[:, sl])]
            o0 = _attend(_keep_lanes(q[:, sl], 0, HD), kvs, HD ** -0.5, bias_ref[0, 2 * s])
            o1 = _attend(_keep_lanes(q[:, sl], HD, HD), kvs, HD ** -0.5, bias_ref[0, 2 * s + 1])
            slabs.append(jnp.where(_lane_half(o0.shape, 0), o0, o1))
        rows.append(jnp.concatenate(slabs, axis=1))
    oa = jnp.concatenate(rows, axis=0)

    bk, bv = bk_ref[...], bv_ref[...]
    cbk = cbk_ref[0, 0].astype(BF16)
    cbv = cbv_ref[0, 0].astype(BF16)
    ob = _mixer_b(bq_ref[...],
                  lambda ks, vs: [(bk[:, ks * LANES:(ks + 1) * LANES], bv[:, vs * LANES:(vs + 1) * LANES]),
                                  (cbk[:, ks * LANES:(ks + 1) * LANES], cbv[:, vs * LANES:(vs + 1) * LANES])],
                  lam, sub_ref[0], lam_init)
    oc = _mixer_c(cq_ref[...], [(ck_ref[...], cv_ref[...]),
                                (cck_ref[0, 0].astype(BF16), ccv_ref[0, 0].astype(BF16))])
    o_ref[...] = jnp.concatenate([oa, ob, oc], axis=1).astype(BF16)


def _lat_attention(l, qkv, caches, bias0, bias1, diff_lambda, sub_tiled):
    aq, ak, av, bq, bk, bv, cq, ck, cv = qkv
    cak, cav, cbk, cbv, cck, ccv = caches
    lam_init = 0.8 - 0.6 * math.exp(-0.3 * l)
    nqb = DEC_SEQ // TOK_TILE
    lat0 = N_CTX // TOK_TILE
    qrow = lambda b, q: (lat0 + b * nqb + q, 0)
    krow = lambda b, q: (N_CTX // DEC_SEQ + b, 0)
    cache = lambda b, q: (b, l, 0, 0)
    nb = nqb - 1
    return pl.pallas_call(
        functools.partial(_lat_attn_kernel, lam_init=lam_init),
        out_shape=jax.ShapeDtypeStruct((N_LAT, D), BF16),
        grid=(DEC_BATCH, nqb),
        in_specs=[pl.BlockSpec((TOK_TILE, W_A), qrow),
                  pl.BlockSpec((DEC_SEQ, W_A), krow),
                  pl.BlockSpec((DEC_SEQ, W_A), krow),
                  pl.BlockSpec((1, 1, PAST, W_A), cache),
                  pl.BlockSpec((1, 1, PAST, W_A), cache),
                  pl.BlockSpec((1, H_A, NA_PAIR, NA_KEYS),
                               lambda b, q: (l * 3 + jnp.where(q == 0, 0, jnp.where(q == nb, 2, 1)), 0, 0, 0)),
                  pl.BlockSpec((1, H_A, NA_PAIR, NA_KEYS),
                               lambda b, q: (l * 3 + jnp.where(q == 0, 0, jnp.where(q == nb, 2, 1)), 0, 0, 0)),
                  pl.BlockSpec((TOK_TILE, W_BQK), qrow),
                  pl.BlockSpec((DEC_SEQ, W_BQK), krow),
                  pl.BlockSpec((DEC_SEQ, W_BV), krow),
                  pl.BlockSpec((1, 1, PAST, W_BQK), cache),
                  pl.BlockSpec((1, 1, PAST, W_BV), cache),
                  pl.BlockSpec((TOK_TILE, W_CQ), qrow),
                  pl.BlockSpec((DEC_SEQ, W_CKV), krow),
                  pl.BlockSpec((DEC_SEQ, W_CKV), krow),
                  pl.BlockSpec((1, 1, PAST, W_CKV), cache),
                  pl.BlockSpec((1, 1, PAST, W_CKV), cache),
                  pl.BlockSpec((1, 4, DQK_B), lambda b, q: (l, 0, 0)),
                  pl.BlockSpec((1, 1, LANES), lambda b, q: (l, 0, 0))],
        out_specs=pl.BlockSpec((TOK_TILE, D), lambda b, q: (b * nqb + q, 0)),
        compiler_params=_cparams(("arbitrary", "arbitrary")),
        name=f"lat_attn_{l}",
    )(aq, ak, av, cak, cav, bias0, bias1, bq, bk, bv, cbk, cbv, cq, ck, cv, cck, ccv,
      diff_lambda, sub_tiled)


def _post_kernel(oc_ref, ol_ref, x_ref, mod_ref, g_ref, w_ref, x1_ref, h2t_ref):
    mod = mod_ref[0]
    gate1, shift2, scale2 = mod[:, 2 * D:3 * D], mod[:, 3 * D:4 * D], mod[:, 4 * D:5 * D]
    o = jnp.where(pl.program_id(0) < CTX_TILES, oc_ref[...], ol_ref[...])
    x1 = x_ref[...] + gate1 * jnp.dot(o, w_ref[0], preferred_element_type=F32)
    x1_ref[...] = x1
    h2 = _rms(x1, g_ref[0]) * (1.0 + scale2) + shift2
    h2t_ref[...] = h2.T.astype(BF16)


def _post(l, o_ctx, o_lat, x, mod, norm_g, w_out16):
    row = lambda i: (i, 0)
    return pl.pallas_call(
        _post_kernel,
        out_shape=(jax.ShapeDtypeStruct((N_TOK, D), F32),
                   jax.ShapeDtypeStruct((D, N_TOK), BF16)),
        grid=(N_TILES,),
        in_specs=[pl.BlockSpec((TOK_TILE, D), lambda i: (jnp.minimum(i, CTX_TILES - 1), 0)),
                  pl.BlockSpec((TOK_TILE, D), lambda i: (jnp.maximum(i - CTX_TILES, 0), 0)),
                  pl.BlockSpec((TOK_TILE, D), row),
                  pl.BlockSpec((1, 1, 6 * D), lambda i: (_mod_row(i), 0, 0)),
                  pl.BlockSpec((1, 1, D), lambda i: (2 * l + 1, 0, 0)),
                  pl.BlockSpec((1, D, D), lambda i: (l, 0, 0))],
        out_specs=(pl.BlockSpec((TOK_TILE, D), row),
                   pl.BlockSpec((D, TOK_TILE), lambda i: (0, i))),
        compiler_params=_cparams(("arbitrary",)),
        name=f"post_{l}",
    )(o_ctx, o_lat, x, mod, norm_g, w_out16)


def _top16(x):
    kio = lax.broadcasted_iota(jnp.int32, x.shape, 0).astype(F32)
    rank = jnp.full(x.shape, 99.0, F32)
    vals = []
    for r in range(PEER_TOPK):
        m = jnp.max(x, axis=0, keepdims=True)
        kmin = jnp.min(jnp.where(x == m, kio, float(N_KEYS)), axis=0, keepdims=True)
        sel = kio == kmin
        rank = jnp.where(sel, float(r), rank)
        x = jnp.where(sel, -jnp.inf, x)
        vals.append(m)
    return rank, jnp.concatenate(vals, axis=0)


def _merge16(pairs):
    io = lax.broadcasted_iota(jnp.int32, pairs[0][0].shape, 0).astype(F32)
    cnt = [jnp.zeros(s1r.shape, F32) for s1r, _ in pairs]
    head = [s1r + s2r[0:1] for s1r, s2r in pairs]
    top = [hd[0:1] for hd in head]
    z = [jnp.zeros_like(t) for t in top]
    for _ in range(PEER_TOPK):
        for i, (s1r, s2r) in enumerate(pairs):
            m = jnp.max(head[i], axis=0, keepdims=True)
            z[i] = z[i] + jnp.exp(m - top[i])
            row = jnp.min(jnp.where(head[i] == m, io, 99.0), axis=0, keepdims=True)
            hit = io == row
            cnt[i] = cnt[i] + jnp.where(hit, 1.0, 0.0)
            nxt_idx = jnp.max(jnp.where(hit, cnt[i], 0.0), axis=0, keepdims=True)
            nxt = jnp.max(jnp.where(io == nxt_idx, s2r, -jnp.inf), axis=0, keepdims=True)
            head[i] = jnp.where(hit, s1r + nxt, head[i])
    return list(zip(cnt, z))


def _topk_kernel(h2t_ref, wqt_ref, keys_ref, rank2_ref, e2_ref, lk_ref, cw_ref, st_ref, sr_ref):
    qt = jnp.dot(wqt_ref[0], h2t_ref[...], preferred_element_type=F32).astype(BF16)
    for hc in range(2 * PEER_HEADS):
        st_ref[hc] = jnp.dot(keys_ref[0, hc], qt[hc * N_KEYS:(hc + 1) * N_KEYS, :],
                             preferred_element_type=F32)

    def head_body(h, carry):
        s1 = st_ref[2 * h]
        s2 = st_ref[2 * h + 1]
        rank1, s1r = _top16(s1)
        rank2, s2r = _top16(s2)
        sr_ref[2 * h] = s1r
        sr_ref[2 * h + 1] = s2r
        rank2_ref[h] = rank2.astype(BF16)
        e2_ref[h] = jnp.exp(s2 - s2r[0:1]).astype(BF16)
        lk_ref[h] = rank1
        cw_ref[h] = jnp.exp(s1 - s1r[0:1])
        return carry

    lax.fori_loop(0, PEER_HEADS, head_body, 0)

    merged = _merge16([(sr_ref[2 * h], sr_ref[2 * h + 1]) for h in range(PEER_HEADS)])
    for h, (cnt, z) in enumerate(merged):
        rank1 = lk_ref[h]
        lk = jnp.zeros_like(rank1)
        for r in range(PEER_TOPK):
            lk = jnp.where(rank1 == float(r), cnt[r:r + 1], lk)
        lk_ref[h] = lk
        cw_ref[h] = cw_ref[h] / z


def _topk(l, h2t, wqt16, keys16):
    blk = pl.BlockSpec((PEER_HEADS, N_KEYS, TOPK_CHUNK), lambda i: (0, 0, i))
    shp = jax.ShapeDtypeStruct((PEER_HEADS, N_KEYS, N_TOK), F32)
    shp16 = jax.ShapeDtypeStruct((PEER_HEADS, N_KEYS, N_TOK), BF16)
    return pl.pallas_call(
        _topk_kernel,
        out_shape=(shp16, shp16, shp, shp),
        grid=(N_TOK // TOPK_CHUNK,),
        in_specs=[pl.BlockSpec((D, TOPK_CHUNK), lambda i: (0, i)),
                  pl.BlockSpec((1, 2 * PEER_HEADS * N_KEYS, D), lambda i: (l, 0, 0)),
                  pl.BlockSpec((1, 2 * PEER_HEADS, N_KEYS, N_KEYS), lambda i: (l, 0, 0, 0))],
        out_specs=(blk, blk, blk, blk),
        scratch_shapes=[pltpu.VMEM((2 * PEER_HEADS, N_KEYS, TOPK_CHUNK), F32),
                        pltpu.VMEM((2 * PEER_HEADS, PEER_TOPK, TOPK_CHUNK), F32)],
        compiler_params=_cparams(("arbitrary",)),
        name=f"topk_{l}",
    )(h2t, wqt16, keys16)


def _gelu(x):
    c = math.sqrt(2.0 / math.pi)
    return 0.5 * x * (1.0 + jnp.tanh(c * (x + 0.044715 * (x * x * x))))


def _expert_kernel(h2t_ref, u_ref, vt_ref, rank2_ref, e2_ref, lk_ref, cw_ref, x1_ref, mod_ref,
                   o_ref, acc_ref, at0_ref, at1_ref):
    j = pl.program_id(1)
    nblk = pl.num_programs(1) - 1

    @pl.when(j == 0)
    def _():
        acc_ref[...] = jnp.zeros_like(acc_ref)
        at1_ref[...] = jnp.zeros_like(at1_ref)

    def step(at_w, at_r):
        at_w[...] = jnp.dot(u_ref[0], h2t_ref[...], preferred_element_type=F32)
        part = None
        per = EXP_CHUNK // N_KEYS
        for c in range(EXP_BLK // EXP_CHUNK):
            ags = []
            for a in range(c * per, (c + 1) * per):
                g = None
                for h in range(PEER_HEADS):
                    lk = lk_ref[h, a:a + 1, :].astype(BF16)
                    cw = cw_ref[h, a:a + 1, :].astype(BF16)
                    contrib = jnp.where(rank2_ref[h] < lk, e2_ref[h] * cw, jnp.zeros((), BF16))
                    g = contrib if g is None else g + contrib
                ags.append(_gelu(at_r[a * N_KEYS:(a + 1) * N_KEYS, :]).astype(BF16) * g)
            d = jnp.dot(vt_ref[0, :, c * EXP_CHUNK:(c + 1) * EXP_CHUNK], jnp.concatenate(ags, axis=0),
                        preferred_element_type=F32)
            part = d if part is None else part + d
        acc_ref[...] += part

    @pl.when(j % 2 == 0)
    def _():
        step(at0_ref, at1_ref)

    @pl.when(j % 2 == 1)
    def _():
        step(at1_ref, at0_ref)

    @pl.when(j == nblk)
    def _():
        gate2 = mod_ref[0][:, 5 * D:6 * D]
        o_ref[...] = x1_ref[...] + gate2 * acc_ref[...].T


def _experts(l, h2t, u16, vt16, rank2, e2, lk, cw, x1, mod):
    nblk = N_EXPERTS // EXP_BLK
    tok = pl.BlockSpec((PEER_HEADS, N_KEYS, EXP_TOK), lambda i, j: (0, 0, i))
    sel = pl.BlockSpec((PEER_HEADS, EXP_BLK // N_KEYS, EXP_TOK), lambda i, j: (0, jnp.maximum(j - 1, 0), i))
    return pl.pallas_call(
        _expert_kernel,
        out_shape=jax.ShapeDtypeStruct((N_TOK, D), F32),
        grid=(N_TOK // EXP_TOK, nblk + 1),
        in_specs=[pl.BlockSpec((D, EXP_TOK), lambda i, j: (0, i)),
                  pl.BlockSpec((1, EXP_BLK, D), lambda i, j: (l, jnp.minimum(j, nblk - 1), 0)),
                  pl.BlockSpec((1, D, EXP_BLK), lambda i, j: (l, 0, jnp.maximum(j - 1, 0))),
                  tok, tok, sel, sel,
                  pl.BlockSpec((EXP_TOK, D), lambda i, j: (i, 0)),
                  pl.BlockSpec((1, 1, 6 * D), lambda i, j: (_mod_row(i * (EXP_TOK // TOK_TILE)), 0, 0))],
        out_specs=pl.BlockSpec((EXP_TOK, D), lambda i, j: (i, 0)),
        scratch_shapes=[pltpu.VMEM((D, EXP_TOK), F32),
                        pltpu.VMEM((EXP_BLK, EXP_TOK), F32),
                        pltpu.VMEM((EXP_BLK, EXP_TOK), F32)],
        compiler_params=_cparams(("arbitrary", "arbitrary")),
        name=f"experts_{l}",
    )(h2t, u16, vt16, rank2, e2, lk, cw, x1, mod)


def _final_kernel(x_ref, g_ref, o_ref):
    o_ref[...] = _rms(x_ref[...], g_ref[...])


def _final_norm(x, g):
    row = lambda i: (i, 0)
    return pl.pallas_call(
        _final_kernel,
        out_shape=jax.ShapeDtypeStruct((N_TOK, D), F32),
        grid=(N_TILES,),
        in_specs=[pl.BlockSpec((TOK_TILE, D), row), pl.BlockSpec((1, D), lambda i: (0, 0))],
        out_specs=pl.BlockSpec((TOK_TILE, D), row),
        compiler_params=_cparams(("arbitrary",)),
        name="final_norm",
    )(x, g.reshape(1, D))


def _rope_tables():
    t = jnp.arange(DEC_SEQ)
    rowp = (t // GRID_W).astype(F32)[:, None]
    colp = (t % GRID_W).astype(F32)[:, None]
    lane = np.arange(LANES)

    def tab(d):
        q = d // 4
        dd = lane % d
        inv = ROPE_THETA ** (-jnp.arange(q, dtype=F32) / q)
        ang = jnp.where((dd < d // 2)[None, :], rowp, colp) * inv[dd % q][None, :]
        sign = np.where((dd % (d // 2)) < q, -1.0, 1.0).astype(np.float32)[None, :]
        cos = jnp.concatenate([jnp.ones((TOK_TILE, LANES), F32), jnp.cos(ang)], axis=0)
        sin = jnp.concatenate([jnp.zeros((TOK_TILE, LANES), F32), jnp.sin(ang) * sign], axis=0)
        return cos, sin

    cb, sb = tab(DQK_B)
    cc, sc = tab(HD)
    return cb, sb, cc, sc


def _na_bias_index():
    def one(j):
        m0 = min(max(j - 2, 0), ROWS // 2 - 5)
        r = 2 * j + np.arange(2)
        rs = np.clip(r - NA_ROWS // 2, 0, ROWS - NA_ROWS)
        krow = 2 * m0 + np.arange(NA_KEYS // GRID_W)
        row_ok = (krow[None, :] >= rs[:, None]) & (krow[None, :] < rs[:, None] + NA_ROWS)
        dr = np.clip(krow[None, :] - r[:, None] + NA_ROWS - 1, 0, 2 * NA_ROWS - 2)
        qc = np.arange(GRID_W)
        cs = np.clip(qc - NA_COLS // 2, 0, GRID_W - NA_COLS)
        col_ok = (qc[None, :] >= cs[:, None]) & (qc[None, :] < cs[:, None] + NA_COLS)
        ok = row_ok[:, None, :, None] & col_ok[None, :, None, :]
        return dr, ok.reshape(NA_PAIR, NA_KEYS)

    mid = ROWS // 4
    first = [one(j) for j in (0, mid, ROWS // 2 - 2)]
    second = [one(j) for j in (1, mid, ROWS // 2 - 1)]
    stack = lambda xs, k: np.stack([x[k] for x in xs])
    return (stack(first, 0), stack(first, 1)), (stack(second, 0), stack(second, 1))


def _na_bias(na_rpb, dr, ok):
    qc = np.arange(GRID_W)
    dc = np.clip(qc[None, :] - qc[:, None], -(NA_COLS - 1), NA_COLS - 1) + NA_COLS - 1
    onehot = (dc[None] == np.arange(2 * NA_COLS - 1)[:, None, None]).astype(np.float32)
    rows = jnp.einsum('lhrd,dn->lhrn', na_rpb, onehot.reshape(2 * NA_COLS - 1, GRID_W * GRID_W),
                      precision=lax.Precision.HIGHEST)
    nkr = NA_KEYS // GRID_W
    t = rows[:, :, dr].reshape(DEPTH, H_A, 3, 2, nkr, GRID_W, GRID_W)
    t = t.transpose(0, 2, 1, 3, 5, 4, 6).reshape(DEPTH, 3, H_A, NA_PAIR, NA_KEYS)
    return jnp.where(ok[None, :, None], t, NEG).reshape(DEPTH * 3, H_A, NA_PAIR, NA_KEYS)


def kernel(x_prompt, x_sample, cache_a_k, cache_a_v, cache_b_k, cache_b_v, cache_c_k, cache_c_v,
           c, c_ctx, w_mod, b_mod, norm_g, w_in, w_out, na_rpb, diff_lambda, diff_subln,
           gqa_qk_norm, peer_wq, peer_keys, peer_u, peer_v, final_g):
    cq0 = 3 * W_A + 2 * W_BQK + W_BV
    g = H_C // KV_C
    w_cq = w_in[:, :, cq0:cq0 + W_CQ].reshape(DEPTH, D, KV_C, g, HD).transpose(0, 1, 3, 2, 4)
    w_in16 = jnp.concatenate([w_in[:, :, :cq0], w_cq.reshape(DEPTH, D, W_CQ), w_in[:, :, cq0 + W_CQ:]],
                             axis=2).astype(BF16)
    oc0 = W_A + W_BV
    w_oc = w_out[:, oc0:].reshape(DEPTH, KV_C, g, HD, D).transpose(0, 2, 1, 3, 4)
    w_out16 = jnp.concatenate([w_out[:, :oc0], w_oc.reshape(DEPTH, W_CQ, D)], axis=1).astype(BF16)
    wqt16 = peer_wq.transpose(0, 2, 1).astype(BF16)
    keys16 = peer_keys.reshape(DEPTH, 2 * PEER_HEADS, N_KEYS, N_KEYS).astype(BF16)
    u16, vt16 = _prep_experts(peer_u, peer_v)

    cond = jnp.concatenate([c_ctx[None, :], c, jnp.zeros((3, D), F32)], axis=0)
    mod_all = _modulation(cond, w_mod, b_mod)[:, :1 + DEC_BATCH].reshape(DEPTH, 1 + DEC_BATCH, 1, 6 * D)
    norm_g3 = norm_g.reshape(2 * DEPTH, 1, D)
    qkn = jnp.tile(gqa_qk_norm, (1, 1, LANES // HD))
    sub_tiled = jnp.tile(diff_subln, (1, LANES // DV_B)).reshape(DEPTH, 1, LANES)
    tabs = _rope_tables()
    (dr0, ok0), (dr1, ok1) = _na_bias_index()
    bias0 = _na_bias(na_rpb, dr0, ok0)
    bias1 = _na_bias(na_rpb, dr1, ok1)
    caches = (cache_a_k.reshape(DEC_BATCH, DEPTH, PAST, W_A), cache_a_v.reshape(DEC_BATCH, DEPTH, PAST, W_A),
              cache_b_k.reshape(DEC_BATCH, DEPTH, PAST, W_BQK), cache_b_v.reshape(DEC_BATCH, DEPTH, PAST, W_BV),
              cache_c_k.reshape(DEC_BATCH, DEPTH, PAST, W_CKV), cache_c_v.reshape(DEC_BATCH, DEPTH, PAST, W_CKV))

    x = jnp.concatenate([x_prompt.reshape(N_CTX, D), x_sample.reshape(N_LAT, D)], axis=0)
    new_kv = []
    for l in range(DEPTH):
        mod = mod_all[l]
        *qkv, kv = _pre(l, x, mod, norm_g3, w_in16, qkn, tabs)
        new_kv.append(kv[:N_CTX].reshape(BATCH, SEQ, KV_WIDTH))
        o_ctx = _ctx_attention(l, qkv, diff_lambda, sub_tiled)
        o_lat = _lat_attention(l, qkv, caches, bias0, bias1, diff_lambda, sub_tiled)
        x1, h2t = _post(l, o_ctx, o_lat, x, mod, norm_g3, w_out16)
        rank2, e2, lk, cw = _topk(l, h2t, wqt16, keys16)
        x = _experts(l, h2t, u16, vt16, rank2, e2, lk, cw, x1, mod)
    y = _final_norm(x, final_g)

    kv = jnp.stack(new_kv, axis=1)
    o = 0
    parts = []
    for w, shp in ((W_A, (H_A, HD)), (W_A, (H_A, HD)), (W_BQK, (H_B, 2, DQK_B)),
                   (W_BV, (H_B, DV_B)), (W_CKV, (KV_C, HD)), (W_CKV, (KV_C, HD))):
        parts.append(kv[..., o:o + w].reshape(BATCH, DEPTH, SEQ, *shp))
        o += w
    return (y[:N_CTX].reshape(BATCH, SEQ, D), y[N_CTX:].reshape(DEC_BATCH, DEC_SEQ, D), *parts)
```

```python
import functools
import math

import numpy as np
import jax
import jax.numpy as jnp
from jax import lax
from jax.experimental import pallas as pl
from jax.experimental.pallas import tpu as pltpu

F32 = jnp.float32
BF16 = jnp.bfloat16

D = 1024
BATCH, SEQ = 16, 256
DEPTH = 4
DEC_BATCH, DEC_SEQ = 4, 2048
PAST = 256
GRID_W = 64
ROWS = DEC_SEQ // GRID_W
HD = 64
H_A, H_B, H_C, KV_C = 6, 4, 6, 2
DQK_B, DV_B = 32, 64
NA_ROWS, NA_COLS = 8, 16
W_A = H_A * HD
W_BQK = H_B * 2 * DQK_B
W_BV = H_B * DV_B
W_CQ = H_C * HD
W_CKV = KV_C * HD
IN_WIDTH = 3 * W_A + 2 * W_BQK + W_BV + W_CQ + 2 * W_CKV
KV_WIDTH = 2 * W_A + W_BQK + W_BV + 2 * W_CKV
N_KEYS = 128
N_EXPERTS = N_KEYS * N_KEYS
PEER_HEADS, PEER_TOPK = 8, 16
ROPE_THETA = 10000.0
EPS = 1e-6

N_CTX = BATCH * SEQ
N_LAT = DEC_BATCH * DEC_SEQ
N_TOK = N_CTX + N_LAT

LANES = 128
TOK_TILE = 256
N_TILES = N_TOK // TOK_TILE
CTX_TILES = N_CTX // TOK_TILE
LAT_TILES_PER_BATCH = DEC_SEQ // TOK_TILE
NA_PAIR = 2 * GRID_W
NA_KEYS = 5 * NA_PAIR
TOPK_CHUNK = 128
EXP_TOK = 512
EXP_BLK = 1024
EXP_CHUNK = 256
NEG = -1e30
VMEM_LIMIT = 56 * 1024 * 1024

_NT = (((1,), (1,)), ((), ()))


def _cparams(sem):
    return pltpu.CompilerParams(dimension_semantics=sem, vmem_limit_bytes=VMEM_LIMIT)


def _mod_row(i):
    return jnp.where(i < CTX_TILES, 0, 1 + (i - CTX_TILES) // LAT_TILES_PER_BATCH)


def _rms(x, g):
    return x * lax.rsqrt(jnp.mean(x * x, axis=-1, keepdims=True) + EPS) * g


def _group_mean_sq(x, width):
    r = lax.broadcasted_iota(jnp.int32, (LANES, LANES), 0) // width
    c = lax.broadcasted_iota(jnp.int32, (LANES, LANES), 1) // width
    bd = jnp.where(r == c, 1.0 / width, 0.0).astype(BF16)
    t = x * x
    hi = t.astype(BF16)
    lo = (t - hi.astype(F32)).astype(BF16)
    return (jnp.dot(hi, bd, preferred_element_type=F32)
            + jnp.dot(lo, bd, preferred_element_type=F32))


def _rope(x, cos, sin_signed, half):
    lane = lax.broadcasted_iota(jnp.int32, x.shape, 1)
    first = (lane % (2 * half)) < half
    rot = jnp.where(first, pltpu.roll(x, LANES - half, 1), pltpu.roll(x, half, 1))
    return x * cos + rot * sin_signed


def _mod_kernel(cond_ref, w_ref, b_ref, o_ref):
    c = cond_ref[...]
    s = c / (1.0 + jnp.exp(-c))
    o_ref[0] = jnp.dot(s.astype(BF16), w_ref[0].astype(BF16),
                       preferred_element_type=F32) + b_ref[0]


def _modulation(cond, w_mod, b_mod):
    ncol = 4
    cw = 6 * D // ncol
    return pl.pallas_call(
        _mod_kernel,
        out_shape=jax.ShapeDtypeStruct((DEPTH, 8, 6 * D), F32),
        grid=(DEPTH, ncol),
        in_specs=[pl.BlockSpec((8, D), lambda l, j: (0, 0)),
                  pl.BlockSpec((1, D, cw), lambda l, j: (l, 0, j)),
                  pl.BlockSpec((1, 1, cw), lambda l, j: (l, 0, j))],
        out_specs=pl.BlockSpec((1, 8, cw), lambda l, j: (l, 0, j)),
        compiler_params=_cparams(("arbitrary", "arbitrary")),
        name="modulation",
    )(cond, w_mod, b_mod.reshape(DEPTH, 1, 6 * D))


def _prep_kernel(u_ref, v_ref, u16_ref, vt_ref):
    u16_ref[0] = u_ref[0].astype(BF16)
    vt_ref[0] = v_ref[0].T.astype(BF16)


def _prep_experts(peer_u, peer_v):
    eb = 512
    return pl.pallas_call(
        _prep_kernel,
        out_shape=(jax.ShapeDtypeStruct((DEPTH, N_EXPERTS, D), BF16),
                   jax.ShapeDtypeStruct((DEPTH, D, N_EXPERTS), BF16)),
        grid=(DEPTH, N_EXPERTS // eb),
        in_specs=[pl.BlockSpec((1, eb, D), lambda l, j: (l, j, 0)),
                  pl.BlockSpec((1, eb, D), lambda l, j: (l, j, 0))],
        out_specs=(pl.BlockSpec((1, eb, D), lambda l, j: (l, j, 0)),
                   pl.BlockSpec((1, D, eb), lambda l, j: (l, 0, j))),
        compiler_params=_cparams(("arbitrary", "arbitrary")),
        name="prep_experts",
    )(peer_u, peer_v)


def _pre_kernel(x_ref, mod_ref, g_ref, w_ref, qkn_ref, cb_ref, sb_ref, cc_ref, sc_ref,
                aq_ref, ak_ref, av_ref, bq_ref, bk_ref, bv_ref, cq_ref, ck_ref, cv_ref, kv_ref):
    x = x_ref[...]
    mod = mod_ref[0]
    shift1, scale1 = mod[:, 0:D], mod[:, D:2 * D]
    h = _rms(x, g_ref[0]) * (1.0 + scale1) + shift1
    r = jnp.dot(h.astype(BF16), w_ref[0], preferred_element_type=F32)

    o = 0
    aq = r[:, o:o + W_A]; o += W_A
    ak = r[:, o:o + W_A]; o += W_A
    av = r[:, o:o + W_A]; o += W_A
    bq = r[:, o:o + W_BQK]; o += W_BQK
    bk = r[:, o:o + W_BQK]; o += W_BQK
    bv = r[:, o:o + W_BV]; o += W_BV
    cq = r[:, o:o + W_CQ]; o += W_CQ
    ck = r[:, o:o + W_CKV]; o += W_CKV
    cv = r[:, o:o + W_CKV]

    cb, sb, cc, sc = cb_ref[...], sb_ref[...], cc_ref[...], sc_ref[...]
    qkn = qkn_ref[0]

    def rope_b(t):
        return jnp.concatenate([_rope(t[:, s:s + LANES], cb, sb, DQK_B // 4)
                                for s in range(0, t.shape[1], LANES)], axis=1)

    def norm_rope_c(t, gain):
        outs = []
        for s in range(0, t.shape[1], LANES):
            ts = t[:, s:s + LANES]
            ts = ts * lax.rsqrt(_group_mean_sq(ts, HD) + EPS) * gain
            outs.append(_rope(ts, cc, sc, HD // 4))
        return jnp.concatenate(outs, axis=1)

    bq = rope_b(bq)
    bk = rope_b(bk)
    cq = norm_rope_c(cq, qkn[0:1])
    ck = norm_rope_c(ck, qkn[1:2])

    aq_ref[...] = aq.astype(BF16)
    ak_ref[...] = ak.astype(BF16)
    av_ref[...] = av.astype(BF16)
    bq_ref[...] = bq.astype(BF16)
    bk_ref[...] = bk.astype(BF16)
    bv_ref[...] = bv.astype(BF16)
    cq_ref[...] = cq.astype(BF16)
    ck_ref[...] = ck.astype(BF16)
    cv_ref[...] = cv.astype(BF16)
    kv_ref[...] = jnp.concatenate([ak, av, bk, bv, ck, cv], axis=1)


def _pre(l, x, mod, norm_g, w_in16, qkn, tabs):
    widths = (W_A, W_A, W_A, W_BQK, W_BQK, W_BV, W_CQ, W_CKV, W_CKV)
    rope_blk = lambda i: (jnp.where(i < CTX_TILES, 0, 1 + (i - CTX_TILES) % LAT_TILES_PER_BATCH), 0)
    row = lambda i: (i, 0)
    kv_row = lambda i: (jnp.minimum(i, CTX_TILES), 0)
    out_shape = tuple(jax.ShapeDtypeStruct((N_TOK, w), BF16) for w in widths) + (
        jax.ShapeDtypeStruct((N_CTX + TOK_TILE, KV_WIDTH), F32),)
    out_specs = tuple(pl.BlockSpec((TOK_TILE, w), row) for w in widths) + (
        pl.BlockSpec((TOK_TILE, KV_WIDTH), kv_row),)
    return pl.pallas_call(
        _pre_kernel,
        out_shape=out_shape,
        grid=(N_TILES,),
        in_specs=[pl.BlockSpec((TOK_TILE, D), row),
                  pl.BlockSpec((1, 1, 6 * D), lambda i: (_mod_row(i), 0, 0)),
                  pl.BlockSpec((1, 1, D), lambda i: (2 * l, 0, 0)),
                  pl.BlockSpec((1, D, IN_WIDTH), lambda i: (l, 0, 0)),
                  pl.BlockSpec((1, 2, LANES), lambda i: (l, 0, 0)),
                  pl.BlockSpec((TOK_TILE, LANES), rope_blk),
                  pl.BlockSpec((TOK_TILE, LANES), rope_blk),
                  pl.BlockSpec((TOK_TILE, LANES), rope_blk),
                  pl.BlockSpec((TOK_TILE, LANES), rope_blk)],
        out_specs=out_specs,
        compiler_params=_cparams(("arbitrary",)),
        name=f"pre_{l}",
    )(x, mod, norm_g, w_in16, qkn, *tabs)


def _lane_half(shape, half):
    lane = lax.broadcasted_iota(jnp.int32, shape, 1)
    return (lane // HD) == half


def _keep_lanes(x, lo, width):
    lane = lax.broadcasted_iota(jnp.int32, x.shape, 1)
    return jnp.where((lane >= lo) & (lane < lo + width), x, jnp.zeros_like(x))


def _attend(q, kvs, scale, bias=None):
    ss = []
    for i, (k, _) in enumerate(kvs):
        s = lax.dot_general(q, k, _NT, preferred_element_type=F32) * scale
        if i == 0 and bias is not None:
            s = s + bias
        ss.append(s)
    m = jnp.max(ss[0], axis=-1, keepdims=True)
    for s in ss[1:]:
        m = jnp.maximum(m, jnp.max(s, axis=-1, keepdims=True))
    o, den = None, None
    for s, (_, v) in zip(ss, kvs):
        p = jnp.exp(s - m)
        ps = jnp.sum(p, axis=-1, keepdims=True)
        po = jnp.dot(p.astype(BF16), v, preferred_element_type=F32)
        o = po if o is None else o + po
        den = ps if den is None else den + ps
    return o / den


def _diff_lambda(dl_ref, lam_init):
    dl = dl_ref[0]
    a = jnp.sum(dl[0:1] * dl[1:2], axis=-1, keepdims=True)
    b = jnp.sum(dl[2:3] * dl[3:4], axis=-1, keepdims=True)
    return jnp.exp(a) - jnp.exp(b) + lam_init


def _mixer_b(bq, kvs_of_slab, lam, sub, lam_init):
    scale = DQK_B ** -0.5
    slabs = []
    for s in range(W_BV // LANES):
        halves = []
        for half in range(2):
            h = 2 * s + half
            maps = []
            for c in range(2):
                hc = 2 * h + c
                q = _keep_lanes(bq[:, (hc // 4) * LANES:(hc // 4 + 1) * LANES], (hc % 4) * DQK_B, DQK_B)
                maps.append(_attend(q, kvs_of_slab(hc // 4, s), scale))
            halves.append(maps[0] - lam * maps[1])
        od = jnp.where(_lane_half(halves[0].shape, 0), halves[0], halves[1])
        od = od * lax.rsqrt(_group_mean_sq(od, DV_B) + EPS) * sub * (1.0 - lam_init)
        slabs.append(od)
    return jnp.concatenate(slabs, axis=1)


def _mixer_c(cq, kvs):
    slabs = []
    for s in range(W_CQ // LANES):
        qs = cq[:, s * LANES:(s + 1) * LANES]
        o0 = _attend(_keep_lanes(qs, 0, HD), kvs, HD ** -0.5)
        o1 = _attend(_keep_lanes(qs, HD, HD), kvs, HD ** -0.5)
        slabs.append(jnp.where(_lane_half(o0.shape, 0), o0, o1))
    return jnp.concatenate(slabs, axis=1)


def _ctx_attn_kernel(aq_ref, ak_ref, av_ref, bq_ref, bk_ref, bv_ref, cq_ref, ck_ref, cv_ref,
                     dl_ref, sub_ref, o_ref, *, lam_init):
    lam = _diff_lambda(dl_ref, lam_init)
    aq, ak, av = aq_ref[...], ak_ref[...], av_ref[...]
    slabs = []
    for s in range(W_A // LANES):
        sl = slice(s * LANES, (s + 1) * LANES)
        kvs = [(ak[:, sl], av[:, sl])]
        o0 = _attend(_keep_lanes(aq[:, sl], 0, HD), kvs, HD ** -0.5)
        o1 = _attend(_keep_lanes(aq[:, sl], HD, HD), kvs, HD ** -0.5)
        slabs.append(jnp.where(_lane_half(o0.shape, 0), o0, o1))
    oa = jnp.concatenate(slabs, axis=1)

    bk, bv = bk_ref[...], bv_ref[...]
    ob = _mixer_b(bq_ref[...],
                  lambda ks, vs: [(bk[:, ks * LANES:(ks + 1) * LANES], bv[:, vs * LANES:(vs + 1) * LANES])],
                  lam, sub_ref[0], lam_init)
    oc = _mixer_c(cq_ref[...], [(ck_ref[...], cv_ref[...])])
    o_ref[...] = jnp.concatenate([oa, ob, oc], axis=1).astype(BF16)


def _ctx_attention(l, qkv, diff_lambda, sub_tiled):
    widths = (W_A, W_A, W_A, W_BQK, W_BQK, W_BV, W_CQ, W_CKV, W_CKV)
    lam_init = 0.8 - 0.6 * math.exp(-0.3 * l)
    return pl.pallas_call(
        functools.partial(_ctx_attn_kernel, lam_init=lam_init),
        out_shape=jax.ShapeDtypeStruct((N_CTX, D), BF16),
        grid=(BATCH,),
        in_specs=[pl.BlockSpec((SEQ, w), lambda b: (b, 0)) for w in widths] + [
            pl.BlockSpec((1, 4, DQK_B), lambda b: (l, 0, 0)),
            pl.BlockSpec((1, 1, LANES), lambda b: (l, 0, 0))],
        out_specs=pl.BlockSpec((SEQ, D), lambda b: (b, 0)),
        compiler_params=_cparams(("arbitrary",)),
        name=f"ctx_attn_{l}",
    )(*qkv, diff_lambda, sub_tiled)


def _lat_attn_kernel(aq_ref, ak_ref, av_ref, cak_ref, cav_ref, bias0_ref, bias1_ref,
                     bq_ref, bk_ref, bv_ref, cbk_ref, cbv_ref,
                     cq_ref, ck_ref, cv_ref, cck_ref, ccv_ref,
                     dl_ref, sub_ref, o_ref, *, lam_init):
    qb = pl.program_id(1)
    lam = _diff_lambda(dl_ref, lam_init)

    cak = cak_ref[0, 0].astype(BF16)
    cav = cav_ref[0, 0].astype(BF16)
    rows = []
    for pidx, bias_ref in enumerate((bias0_ref, bias1_ref)):
        j = 2 * qb + pidx
        start = pl.multiple_of(jnp.clip(j - 2, 0, ROWS // 2 - 5) * NA_PAIR, NA_PAIR)
        kwin = ak_ref[pl.ds(start, NA_KEYS), :]
        vwin = av_ref[pl.ds(start, NA_KEYS), :]
        q = aq_ref[pidx * NA_PAIR:(pidx + 1) * NA_PAIR, :]
        slabs = []
        for s in range(W_A // LANES):
            sl = slice(s * LANES, (s + 1) * LANES)
            kvs = [(kwin[:, sl], vwin[:, sl]), (cak[:, sl], cav[:, sl])]
            o0 = _attend(_keep_lanes(q[:, sl], 0, HD), kvs, HD ** -0.5, bias_ref[0, 2 * s])
            o1 = _attend(_keep_lanes(q[:, sl], HD, HD), kvs, HD ** -0.5, bias_ref[0, 2 * s + 1])
            slabs.append(jnp.where(_lane_half(o0.shape, 0), o0, o1))
        rows.append(jnp.concatenate(slabs, axis=1))
    oa = jnp.concatenate(rows, axis=0)

    bk, bv = bk_ref[...], bv_ref[...]
    cbk = cbk_ref[0, 0].astype(BF16)
    cbv = cbv_ref[0, 0].astype(BF16)
    ob = _mixer_b(bq_ref[...],
                  lambda ks, vs: [(bk[:, ks * LANES:(ks + 1) * LANES], bv[:, vs * LANES:(vs + 1) * LANES]),
                                  (cbk[:, ks * LANES:(ks + 1) * LANES], cbv[:, vs * LANES:(vs + 1) * LANES])],
                  lam, sub_ref[0], lam_init)
    oc = _mixer_c(cq_ref[...], [(ck_ref[...], cv_ref[...]),
                                (cck_ref[0, 0].astype(BF16), ccv_ref[0, 0].astype(BF16))])
    o_ref[...] = jnp.concatenate([oa, ob, oc], axis=1).astype(BF16)


def _lat_attention(l, qkv, caches, bias0, bias1, diff_lambda, sub_tiled):
    aq, ak, av, bq, bk, bv, cq, ck, cv = qkv
    cak, cav, cbk, cbv, cck, ccv = caches
    lam_init = 0.8 - 0.6 * math.exp(-0.3 * l)
    nqb = DEC_SEQ // TOK_TILE
    lat0 = N_CTX // TOK_TILE
    qrow = lambda b, q: (lat0 + b * nqb + q, 0)
    krow = lambda b, q: (N_CTX // DEC_SEQ + b, 0)
    cache = lambda b, q: (b, l, 0, 0)
    nb = nqb - 1
    return pl.pallas_call(
        functools.partial(_lat_attn_kernel, lam_init=lam_init),
        out_shape=jax.ShapeDtypeStruct((N_LAT, D), BF16),
        grid=(DEC_BATCH, nqb),
        in_specs=[pl.BlockSpec((TOK_TILE, W_A), qrow),
                  pl.BlockSpec((DEC_SEQ, W_A), krow),
                  pl.BlockSpec((DEC_SEQ, W_A), krow),
                  pl.BlockSpec((1, 1, PAST, W_A), cache),
                  pl.BlockSpec((1, 1, PAST, W_A), cache),
                  pl.BlockSpec((1, H_A, NA_PAIR, NA_KEYS),
                               lambda b, q: (l * 3 + jnp.where(q == 0, 0, jnp.where(q == nb, 2, 1)), 0, 0, 0)),
                  pl.BlockSpec((1, H_A, NA_PAIR, NA_KEYS),
                               lambda b, q: (l * 3 + jnp.where(q == 0, 0, jnp.where(q == nb, 2, 1)), 0, 0, 0)),
                  pl.BlockSpec((TOK_TILE, W_BQK), qrow),
                  pl.BlockSpec((DEC_SEQ, W_BQK), krow),
                  pl.BlockSpec((DEC_SEQ, W_BV), krow),
                  pl.BlockSpec((1, 1, PAST, W_BQK), cache),
                  pl.BlockSpec((1, 1, PAST, W_BV), cache),
                  pl.BlockSpec((TOK_TILE, W_CQ), qrow),
                  pl.BlockSpec((DEC_SEQ, W_CKV), krow),
                  pl.BlockSpec((DEC_SEQ, W_CKV), krow),
                  pl.BlockSpec((1, 1, PAST, W_CKV), cache),
                  pl.BlockSpec((1, 1, PAST, W_CKV), cache),
                  pl.BlockSpec((1, 4, DQK_B), lambda b, q: (l, 0, 0)),
                  pl.BlockSpec((1, 1, LANES), lambda b, q: (l, 0, 0))],
        out_specs=pl.BlockSpec((TOK_TILE, D), lambda b, q: (b * nqb + q, 0)),
        compiler_params=_cparams(("arbitrary", "arbitrary")),
        name=f"lat_attn_{l}",
    )(aq, ak, av, cak, cav, bias0, bias1, bq, bk, bv, cbk, cbv, cq, ck, cv, cck, ccv,
      diff_lambda, sub_tiled)


def _post_kernel(oc_ref, ol_ref, x_ref, mod_ref, g_ref, w_ref, x1_ref, h2t_ref):
    mod = mod_ref[0]
    gate1, shift2, scale2 = mod[:, 2 * D:3 * D], mod[:, 3 * D:4 * D], mod[:, 4 * D:5 * D]
    o = jnp.where(pl.program_id(0) < CTX_TILES, oc_ref[...], ol_ref[...])
    x1 = x_ref[...] + gate1 * jnp.dot(o, w_ref[0], preferred_element_type=F32)
    x1_ref[...] = x1
    h2 = _rms(x1, g_ref[0]) * (1.0 + scale2) + shift2
    h2t_ref[...] = h2.T.astype(BF16)


def _post(l, o_ctx, o_lat, x, mod, norm_g, w_out16):
    row = lambda i: (i, 0)
    return pl.pallas_call(
        _post_kernel,
        out_shape=(jax.ShapeDtypeStruct((N_TOK, D), F32),
                   jax.ShapeDtypeStruct((D, N_TOK), BF16)),
        grid=(N_TILES,),
        in_specs=[pl.BlockSpec((TOK_TILE, D), lambda i: (jnp.minimum(i, CTX_TILES - 1), 0)),
                  pl.BlockSpec((TOK_TILE, D), lambda i: (jnp.maximum(i - CTX_TILES, 0), 0)),
                  pl.BlockSpec((TOK_TILE, D), row),
                  pl.BlockSpec((1, 1, 6 * D), lambda i: (_mod_row(i), 0, 0)),
                  pl.BlockSpec((1, 1, D), lambda i: (2 * l + 1, 0, 0)),
                  pl.BlockSpec((1, D, D), lambda i: (l, 0, 0))],
        out_specs=(pl.BlockSpec((TOK_TILE, D), row),
                   pl.BlockSpec((D, TOK_TILE), lambda i: (0, i))),
        compiler_params=_cparams(("arbitrary",)),
        name=f"post_{l}",
    )(o_ctx, o_lat, x, mod, norm_g, w_out16)


def _top16(x):
    kio = lax.broadcasted_iota(jnp.int32, x.shape, 0).astype(F32)
    rank = jnp.full(x.shape, 99.0, F32)
    vals = []
    for r in range(PEER_TOPK):
        m = jnp.max(x, axis=0, keepdims=True)
        kmin = jnp.min(jnp.where(x == m, kio, float(N_KEYS)), axis=0, keepdims=True)
        sel = kio == kmin
        rank = jnp.where(sel, float(r), rank)
        x = jnp.where(sel, -jnp.inf, x)
        vals.append(m)
    return rank, jnp.concatenate(vals, axis=0)


def _merge16(pairs):
    io = lax.broadcasted_iota(jnp.int32, pairs[0][0].shape, 0).astype(F32)
    cnt = [jnp.zeros(s1r.shape, F32) for s1r, _ in pairs]
    head = [s1r + s2r[0:1] for s1r, s2r in pairs]
    top = [hd[0:1] for hd in head]
    z = [jnp.zeros_like(t) for t in top]
    for _ in range(PEER_TOPK):
        for i, (s1r, s2r) in enumerate(pairs):
            m = jnp.max(head[i], axis=0, keepdims=True)
            z[i] = z[i] + jnp.exp(m - top[i])
            row = jnp.min(jnp.where(head[i] == m, io, 99.0), axis=0, keepdims=True)
            hit = io == row
            cnt[i] = cnt[i] + jnp.where(hit, 1.0, 0.0)
            nxt_idx = jnp.max(jnp.where(hit, cnt[i], 0.0), axis=0, keepdims=True)
            nxt = jnp.max(jnp.where(io == nxt_idx, s2r, -jnp.inf), axis=0, keepdims=True)
            head[i] = jnp.where(hit, s1r + nxt, head[i])
    return list(zip(cnt, z))


def _topk_kernel(h2t_ref, wqt_ref, keys_ref, rank2_ref, e2_ref, lk_ref, cw_ref, st_ref, sr_ref):
    qt = jnp.dot(wqt_ref[0], h2t_ref[...], preferred_element_type=F32).astype(BF16)
    for hc in range(2 * PEER_HEADS):
        st_ref[hc] = jnp.dot(keys_ref[0, hc], qt[hc * N_KEYS:(hc + 1) * N_KEYS, :],
                             preferred_element_type=F32)

    def head_body(h, carry):
        s1 = st_ref[2 * h]
        s2 = st_ref[2 * h + 1]
        rank1, s1r = _top16(s1)
        rank2, s2r = _top16(s2)
        sr_ref[2 * h] = s1r
        sr_ref[2 * h + 1] = s2r
        rank2_ref[h] = rank2.astype(BF16)
        e2_ref[h] = jnp.exp(s2 - s2r[0:1]).astype(BF16)
        lk_ref[h] = rank1
        cw_ref[h] = jnp.exp(s1 - s1r[0:1])
        return carry

    lax.fori_loop(0, PEER_HEADS, head_body, 0, unroll=2)

    merged = _merge16([(sr_ref[2 * h], sr_ref[2 * h + 1]) for h in range(PEER_HEADS)])
    for h, (cnt, z) in enumerate(merged):
        rank1 = lk_ref[h]
        lk = jnp.zeros_like(rank1)
        for r in range(PEER_TOPK):
            lk = jnp.where(rank1 == float(r), cnt[r:r + 1], lk)
        lk_ref[h] = lk
        cw_ref[h] = cw_ref[h] / z


def _topk(l, h2t, wqt16, keys16):
    blk = pl.BlockSpec((PEER_HEADS, N_KEYS, TOPK_CHUNK), lambda i: (0, 0, i))
    shp = jax.ShapeDtypeStruct((PEER_HEADS, N_KEYS, N_TOK), F32)
    shp16 = jax.ShapeDtypeStruct((PEER_HEADS, N_KEYS, N_TOK), BF16)
    return pl.pallas_call(
        _topk_kernel,
        out_shape=(shp16, shp16, shp, shp),
        grid=(N_TOK // TOPK_CHUNK,),
        in_specs=[pl.BlockSpec((D, TOPK_CHUNK), lambda i: (0, i)),
                  pl.BlockSpec((1, 2 * PEER_HEADS * N_KEYS, D), lambda i: (l, 0, 0)),
                  pl.BlockSpec((1, 2 * PEER_HEADS, N_KEYS, N_KEYS), lambda i: (l, 0, 0, 0))],
        out_specs=(blk, blk, blk, blk),
        scratch_shapes=[pltpu.VMEM((2 * PEER_HEADS, N_KEYS, TOPK_CHUNK), F32),
                        pltpu.VMEM((2 * PEER_HEADS, PEER_TOPK, TOPK_CHUNK), F32)],
        compiler_params=_cparams(("arbitrary",)),
        name=f"topk_{l}",
    )(h2t, wqt16, keys16)


def _gelu(x):
    c = math.sqrt(2.0 / math.pi)
    return 0.5 * x * (1.0 + jnp.tanh(c * (x + 0.044715 * (x * x * x))))


def _expert_kernel(h2t_ref, u_ref, vt_ref, rank2_ref, e2_ref, lk_ref, cw_ref, x1_ref, mod_ref,
                   o_ref, acc_ref, at0_ref, at1_ref):
    j = pl.program_id(1)
    nblk = pl.num_programs(1) - 1

    @pl.when(j == 0)
    def _():
        acc_ref[...] = jnp.zeros_like(acc_ref)
        at1_ref[...] = jnp.zeros_like(at1_ref)

    def step(at_w, at_r):
        at_w[...] = jnp.dot(u_ref[0], h2t_ref[...], preferred_element_type=F32)
        part = None
        per = EXP_CHUNK // N_KEYS
        for c in range(EXP_BLK // EXP_CHUNK):
            ags = []
            for a in range(c * per, (c + 1) * per):
                g = None
                for h in range(PEER_HEADS):
                    lk = lk_ref[h, a:a + 1, :].astype(BF16)
                    cw = cw_ref[h, a:a + 1, :].astype(BF16)
                    contrib = jnp.where(rank2_ref[h] < lk, e2_ref[h] * cw, jnp.zeros((), BF16))
                    g = contrib if g is None else g + contrib
                ags.append(_gelu(at_r[a * N_KEYS:(a + 1) * N_KEYS, :]).astype(BF16) * g)
            d = jnp.dot(vt_ref[0, :, c * EXP_CHUNK:(c + 1) * EXP_CHUNK], jnp.concatenate(ags, axis=0),
                        preferred_element_type=F32)
            part = d if part is None else part + d
        acc_ref[...] += part

    @pl.when(j % 2 == 0)
    def _():
        step(at0_ref, at1_ref)

    @pl.when(j % 2 == 1)
    def _():
        step(at1_ref, at0_ref)

    @pl.when(j == nblk)
    def _():
        gate2 = mod_ref[0][:, 5 * D:6 * D]
        o_ref[...] = x1_ref[...] + gate2 * acc_ref[...].T


def _experts(l, h2t, u16, vt16, rank2, e2, lk, cw, x1, mod):
    nblk = N_EXPERTS // EXP_BLK
    tok = pl.BlockSpec((PEER_HEADS, N_KEYS, EXP_TOK), lambda i, j: (0, 0, i))
    sel = pl.BlockSpec((PEER_HEADS, EXP_BLK // N_KEYS, EXP_TOK), lambda i, j: (0, jnp.maximum(j - 1, 0), i))
    return pl.pallas_call(
        _expert_kernel,
        out_shape=jax.ShapeDtypeStruct((N_TOK, D), F32),
        grid=(N_TOK // EXP_TOK, nblk + 1),
        in_specs=[pl.BlockSpec((D, EXP_TOK), lambda i, j: (0, i)),
                  pl.BlockSpec((1, EXP_BLK, D), lambda i, j: (l, jnp.minimum(j, nblk - 1), 0)),
                  pl.BlockSpec((1, D, EXP_BLK), lambda i, j: (l, 0, jnp.maximum(j - 1, 0))),
                  tok, tok, sel, sel,
                  pl.BlockSpec((EXP_TOK, D), lambda i, j: (i, 0)),
                  pl.BlockSpec((1, 1, 6 * D), lambda i, j: (_mod_row(i * (EXP_TOK // TOK_TILE)), 0, 0))],
        out_specs=pl.BlockSpec((EXP_TOK, D), lambda i, j: (i, 0)),
        scratch_shapes=[pltpu.VMEM((D, EXP_TOK), F32),
                        pltpu.VMEM((EXP_BLK, EXP_TOK), F32),
                        pltpu.VMEM((EXP_BLK, EXP_TOK), F32)],
        compiler_params=_cparams(("arbitrary", "arbitrary")),
        name=f"experts_{l}",
    )(h2t, u16, vt16, rank2, e2, lk, cw, x1, mod)


def _final_kernel(x_ref, g_ref, o_ref):
    o_ref[...] = _rms(x_ref[...], g_ref[...])


def _final_norm(x, g):
    row = lambda i: (i, 0)
    return pl.pallas_call(
        _final_kernel,
        out_shape=jax.ShapeDtypeStruct((N_TOK, D), F32),
        grid=(N_TILES,),
        in_specs=[pl.BlockSpec((TOK_TILE, D), row), pl.BlockSpec((1, D), lambda i: (0, 0))],
        out_specs=pl.BlockSpec((TOK_TILE, D), row),
        compiler_params=_cparams(("arbitrary",)),
        name="final_norm",
    )(x, g.reshape(1, D))


def _rope_tables():
    t = jnp.arange(DEC_SEQ)
    rowp = (t // GRID_W).astype(F32)[:, None]
    colp = (t % GRID_W).astype(F32)[:, None]
    lane = np.arange(LANES)

    def tab(d):
        q = d // 4
        dd = lane % d
        inv = ROPE_THETA ** (-jnp.arange(q, dtype=F32) / q)
        ang = jnp.where((dd < d // 2)[None, :], rowp, colp) * inv[dd % q][None, :]
        sign = np.where((dd % (d // 2)) < q, -1.0, 1.0).astype(np.float32)[None, :]
        cos = jnp.concatenate([jnp.ones((TOK_TILE, LANES), F32), jnp.cos(ang)], axis=0)
        sin = jnp.concatenate([jnp.zeros((TOK_TILE, LANES), F32), jnp.sin(ang) * sign], axis=0)
        return cos, sin

    cb, sb = tab(DQK_B)
    cc, sc = tab(HD)
    return cb, sb, cc, sc


def _na_bias_index():
    def one(j):
        m0 = min(max(j - 2, 0), ROWS // 2 - 5)
        r = 2 * j + np.arange(2)
        rs = np.clip(r - NA_ROWS // 2, 0, ROWS - NA_ROWS)
        krow = 2 * m0 + np.arange(NA_KEYS // GRID_W)
        row_ok = (krow[None, :] >= rs[:, None]) & (krow[None, :] < rs[:, None] + NA_ROWS)
        dr = np.clip(krow[None, :] - r[:, None] + NA_ROWS - 1, 0, 2 * NA_ROWS - 2)
        qc = np.arange(GRID_W)
        cs = np.clip(qc - NA_COLS // 2, 0, GRID_W - NA_COLS)
        col_ok = (qc[None, :] >= cs[:, None]) & (qc[None, :] < cs[:, None] + NA_COLS)
        ok = row_ok[:, None, :, None] & col_ok[None, :, None, :]
        return dr, ok.reshape(NA_PAIR, NA_KEYS)

    mid = ROWS // 4
    first = [one(j) for j in (0, mid, ROWS // 2 - 2)]
    second = [one(j) for j in (1, mid, ROWS // 2 - 1)]
    stack = lambda xs, k: np.stack([x[k] for x in xs])
    return (stack(first, 0), stack(first, 1)), (stack(second, 0), stack(second, 1))


def _na_bias(na_rpb, dr, ok):
    qc = np.arange(GRID_W)
    dc = np.clip(qc[None, :] - qc[:, None], -(NA_COLS - 1), NA_COLS - 1) + NA_COLS - 1
    onehot = (dc[None] == np.arange(2 * NA_COLS - 1)[:, None, None]).astype(np.float32)
    rows = jnp.einsum('lhrd,dn->lhrn', na_rpb, onehot.reshape(2 * NA_COLS - 1, GRID_W * GRID_W),
                      precision=lax.Precision.HIGHEST)
    nkr = NA_KEYS // GRID_W
    t = rows[:, :, dr].reshape(DEPTH, H_A, 3, 2, nkr, GRID_W, GRID_W)
    t = t.transpose(0, 2, 1, 3, 5, 4, 6).reshape(DEPTH, 3, H_A, NA_PAIR, NA_KEYS)
    return jnp.where(ok[None, :, None], t, NEG).reshape(DEPTH * 3, H_A, NA_PAIR, NA_KEYS)


def kernel(x_prompt, x_sample, cache_a_k, cache_a_v, cache_b_k, cache_b_v, cache_c_k, cache_c_v,
           c, c_ctx, w_mod, b_mod, norm_g, w_in, w_out, na_rpb, diff_lambda, diff_subln,
           gqa_qk_norm, peer_wq, peer_keys, peer_u, peer_v, final_g):
    cq0 = 3 * W_A + 2 * W_BQK + W_BV
    g = H_C // KV_C
    w_cq = w_in[:, :, cq0:cq0 + W_CQ].reshape(DEPTH, D, KV_C, g, HD).transpose(0, 1, 3, 2, 4)
    w_in16 = jnp.concatenate([w_in[:, :, :cq0], w_cq.reshape(DEPTH, D, W_CQ), w_in[:, :, cq0 + W_CQ:]],
                             axis=2).astype(BF16)
    oc0 = W_A + W_BV
    w_oc = w_out[:, oc0:].reshape(DEPTH, KV_C, g, HD, D).transpose(0, 2, 1, 3, 4)
    w_out16 = jnp.concatenate([w_out[:, :oc0], w_oc.reshape(DEPTH, W_CQ, D)], axis=1).astype(BF16)
    wqt16 = peer_wq.transpose(0, 2, 1).astype(BF16)
    keys16 = peer_keys.reshape(DEPTH, 2 * PEER_HEADS, N_KEYS, N_KEYS).astype(BF16)
    u16, vt16 = _prep_experts(peer_u, peer_v)

    cond = jnp.concatenate([c_ctx[None, :], c, jnp.zeros((3, D), F32)], axis=0)
    mod_all = _modulation(cond, w_mod, b_mod)[:, :1 + DEC_BATCH].reshape(DEPTH, 1 + DEC_BATCH, 1, 6 * D)
    norm_g3 = norm_g.reshape(2 * DEPTH, 1, D)
    qkn = jnp.tile(gqa_qk_norm, (1, 1, LANES // HD))
    sub_tiled = jnp.tile(diff_subln, (1, LANES // DV_B)).reshape(DEPTH, 1, LANES)
    tabs = _rope_tables()
    (dr0, ok0), (dr1, ok1) = _na_bias_index()
    bias0 = _na_bias(na_rpb, dr0, ok0)
    bias1 = _na_bias(na_rpb, dr1, ok1)
    caches = (cache_a_k.reshape(DEC_BATCH, DEPTH, PAST, W_A), cache_a_v.reshape(DEC_BATCH, DEPTH, PAST, W_A),
              cache_b_k.reshape(DEC_BATCH, DEPTH, PAST, W_BQK), cache_b_v.reshape(DEC_BATCH, DEPTH, PAST, W_BV),
              cache_c_k.reshape(DEC_BATCH, DEPTH, PAST, W_CKV), cache_c_v.reshape(DEC_BATCH, DEPTH, PAST, W_CKV))

    x = jnp.concatenate([x_prompt.reshape(N_CTX, D), x_sample.reshape(N_LAT, D)], axis=0)
    new_kv = []
    for l in range(DEPTH):
        mod = mod_all[l]
        *qkv, kv = _pre(l, x, mod, norm_g3, w_in16, qkn, tabs)
        new_kv.append(kv[:N_CTX].reshape(BATCH, SEQ, KV_WIDTH))
        o_ctx = _ctx_attention(l, qkv, diff_lambda, sub_tiled)
        o_lat = _lat_attention(l, qkv, caches, bias0, bias1, diff_lambda, sub_tiled)
        x1, h2t = _post(l, o_ctx, o_lat, x, mod, norm_g3, w_out16)
        rank2, e2, lk, cw = _topk(l, h2t, wqt16, keys16)
        x = _experts(l, h2t, u16, vt16, rank2, e2, lk, cw, x1, mod)
    y = _final_norm(x, final_g)

    kv = jnp.stack(new_kv, axis=1)
    o = 0
    parts = []
    for w, shp in ((W_A, (H_A, HD)), (W_A, (H_A, HD)), (W_BQK, (H_B, 2, DQK_B)),
                   (W_BV, (H_B, DV_B)), (W_CKV, (KV_C, HD)), (W_CKV, (KV_C, HD))):
        parts.append(kv[..., o:o + w].reshape(BATCH, DEPTH, SEQ, *shp))
        o += w
    return (y[:N_CTX].reshape(BATCH, SEQ, D), y[N_CTX:].reshape(DEC_BATCH, DEC_SEQ, D), *parts)
```

```python
import functools
import math

import numpy as np
import jax
import jax.numpy as jnp
from jax import lax
from jax.experimental import pallas as pl
from jax.experimental.pallas import tpu as pltpu

F32 = jnp.float32
BF16 = jnp.bfloat16

D = 1024
BATCH, SEQ = 16, 256
DEPTH = 4
DEC_BATCH, DEC_SEQ = 4, 2048
PAST = 256
GRID_W = 64
ROWS = DEC_SEQ // GRID_W
HD = 64
H_A, H_B, H_C, KV_C = 6, 4, 6, 2
DQK_B, DV_B = 32, 64
NA_ROWS, NA_COLS = 8, 16
W_A = H_A * HD
W_BQK = H_B * 2 * DQK_B
W_BV = H_B * DV_B
W_CQ = H_C * HD
W_CKV = KV_C * HD
IN_WIDTH = 3 * W_A + 2 * W_BQK + W_BV + W_CQ + 2 * W_CKV
KV_WIDTH = 2 * W_A + W_BQK + W_BV + 2 * W_CKV
N_KEYS = 128
N_EXPERTS = N_KEYS * N_KEYS
PEER_HEADS, PEER_TOPK = 8, 16
ROPE_THETA = 10000.0
EPS = 1e-6

N_CTX = BATCH * SEQ
N_LAT = DEC_BATCH * DEC_SEQ
N_TOK = N_CTX + N_LAT

LANES = 128
TOK_TILE = 256
N_TILES = N_TOK // TOK_TILE
CTX_TILES = N_CTX // TOK_TILE
LAT_TILES_PER_BATCH = DEC_SEQ // TOK_TILE
NA_PAIR = 2 * GRID_W
NA_KEYS = 5 * NA_PAIR
TOPK_CHUNK = 128
EXP_TOK = 512
EXP_BLK = 1024
EXP_CHUNK = 512
NEG = -1e30
VMEM_LIMIT = 56 * 1024 * 1024

_NT = (((1,), (1,)), ((), ()))


def _cparams(sem):
    return pltpu.CompilerParams(dimension_semantics=sem, vmem_limit_bytes=VMEM_LIMIT)


def _mod_row(i):
    return jnp.where(i < CTX_TILES, 0, 1 + (i - CTX_TILES) // LAT_TILES_PER_BATCH)


def _rms(x, g):
    return x * lax.rsqrt(jnp.mean(x * x, axis=-1, keepdims=True) + EPS) * g


def _group_mean_sq(x, width):
    r = lax.broadcasted_iota(jnp.int32, (LANES, LANES), 0) // width
    c = lax.broadcasted_iota(jnp.int32, (LANES, LANES), 1) // width
    bd = jnp.where(r == c, 1.0 / width, 0.0).astype(BF16)
    t = x * x
    hi = t.astype(BF16)
    lo = (t - hi.astype(F32)).astype(BF16)
    return (jnp.dot(hi, bd, preferred_element_type=F32)
            + jnp.dot(lo, bd, preferred_element_type=F32))


def _rope(x, cos, sin_signed, half):
    lane = lax.broadcasted_iota(jnp.int32, x.shape, 1)
    first = (lane % (2 * half)) < half
    rot = jnp.where(first, pltpu.roll(x, LANES - half, 1), pltpu.roll(x, half, 1))
    return x * cos + rot * sin_signed


def _mod_kernel(cond_ref, w_ref, b_ref, o_ref):
    c = cond_ref[...]
    s = c / (1.0 + jnp.exp(-c))
    o_ref[0] = jnp.dot(s.astype(BF16), w_ref[0].astype(BF16),
                       preferred_element_type=F32) + b_ref[0]


def _modulation(cond, w_mod, b_mod):
    ncol = 4
    cw = 6 * D // ncol
    return pl.pallas_call(
        _mod_kernel,
        out_shape=jax.ShapeDtypeStruct((DEPTH, 8, 6 * D), F32),
        grid=(DEPTH, ncol),
        in_specs=[pl.BlockSpec((8, D), lambda l, j: (0, 0)),
                  pl.BlockSpec((1, D, cw), lambda l, j: (l, 0, j)),
                  pl.BlockSpec((1, 1, cw), lambda l, j: (l, 0, j))],
        out_specs=pl.BlockSpec((1, 8, cw), lambda l, j: (l, 0, j)),
        compiler_params=_cparams(("arbitrary", "arbitrary")),
        name="modulation",
    )(cond, w_mod, b_mod.reshape(DEPTH, 1, 6 * D))


def _prep_kernel(u_ref, v_ref, u16_ref, vt_ref):
    u16_ref[0] = u_ref[0].astype(BF16)
    vt_ref[0] = v_ref[0].T.astype(BF16)


def _prep_experts(peer_u, peer_v):
    eb = 512
    return pl.pallas_call(
        _prep_kernel,
        out_shape=(jax.ShapeDtypeStruct((DEPTH, N_EXPERTS, D), BF16),
                   jax.ShapeDtypeStruct((DEPTH, D, N_EXPERTS), BF16)),
        grid=(DEPTH, N_EXPERTS // eb),
        in_specs=[pl.BlockSpec((1, eb, D), lambda l, j: (l, j, 0)),
                  pl.BlockSpec((1, eb, D), lambda l, j: (l, j, 0))],
        out_specs=(pl.BlockSpec((1, eb, D), lambda l, j: (l, j, 0)),
                   pl.BlockSpec((1, D, eb), lambda l, j: (l, 0, j))),
        compiler_params=_cparams(("arbitrary", "arbitrary")),
        name="prep_experts",
    )(peer_u, peer_v)


def _pre_kernel(x_ref, mod_ref, g_ref, w_ref, qkn_ref, cb_ref, sb_ref, cc_ref, sc_ref,
                aq_ref, ak_ref, av_ref, bq_ref, bk_ref, bv_ref, cq_ref, ck_ref, cv_ref, kv_ref):
    x = x_ref[...]
    mod = mod_ref[0]
    shift1, scale1 = mod[:, 0:D], mod[:, D:2 * D]
    h = _rms(x, g_ref[0]) * (1.0 + scale1) + shift1
    r = jnp.dot(h.astype(BF16), w_ref[0], preferred_element_type=F32)

    o = 0
    aq = r[:, o:o + W_A]; o += W_A
    ak = r[:, o:o + W_A]; o += W_A
    av = r[:, o:o + W_A]; o += W_A
    bq = r[:, o:o + W_BQK]; o += W_BQK
    bk = r[:, o:o + W_BQK]; o += W_BQK
    bv = r[:, o:o + W_BV]; o += W_BV
    cq = r[:, o:o + W_CQ]; o += W_CQ
    ck = r[:, o:o + W_CKV]; o += W_CKV
    cv = r[:, o:o + W_CKV]

    cb, sb, cc, sc = cb_ref[...], sb_ref[...], cc_ref[...], sc_ref[...]
    qkn = qkn_ref[0]

    def rope_b(t):
        return jnp.concatenate([_rope(t[:, s:s + LANES], cb, sb, DQK_B // 4)
                                for s in range(0, t.shape[1], LANES)], axis=1)

    def norm_rope_c(t, gain):
        outs = []
        for s in range(0, t.shape[1], LANES):
            ts = t[:, s:s + LANES]
            ts = ts * lax.rsqrt(_group_mean_sq(ts, HD) + EPS) * gain
            outs.append(_rope(ts, cc, sc, HD // 4))
        return jnp.concatenate(outs, axis=1)

    bq = rope_b(bq)
    bk = rope_b(bk)
    cq = norm_rope_c(cq, qkn[0:1])
    ck = norm_rope_c(ck, qkn[1:2])

    aq_ref[...] = aq.astype(BF16)
    ak_ref[...] = ak.astype(BF16)
    av_ref[...] = av.astype(BF16)
    bq_ref[...] = bq.astype(BF16)
    bk_ref[...] = bk.astype(BF16)
    bv_ref[...] = bv.astype(BF16)
    cq_ref[...] = cq.astype(BF16)
    ck_ref[...] = ck.astype(BF16)
    cv_ref[...] = cv.astype(BF16)
    kv_ref[...] = jnp.concatenate([ak, av, bk, bv, ck, cv], axis=1)


def _pre(l, x, mod, norm_g, w_in16, qkn, tabs):
    widths = (W_A, W_A, W_A, W_BQK, W_BQK, W_BV, W_CQ, W_CKV, W_CKV)
    rope_blk = lambda i: (jnp.where(i < CTX_TILES, 0, 1 + (i - CTX_TILES) % LAT_TILES_PER_BATCH), 0)
    row = lambda i: (i, 0)
    kv_row = lambda i: (jnp.minimum(i, CTX_TILES), 0)
    out_shape = tuple(jax.ShapeDtypeStruct((N_TOK, w), BF16) for w in widths) + (
        jax.ShapeDtypeStruct((N_CTX + TOK_TILE, KV_WIDTH), F32),)
    out_specs = tuple(pl.BlockSpec((TOK_TILE, w), row) for w in widths) + (
        pl.BlockSpec((TOK_TILE, KV_WIDTH), kv_row),)
    return pl.pallas_call(
        _pre_kernel,
        out_shape=out_shape,
        grid=(N_TILES,),
        in_specs=[pl.BlockSpec((TOK_TILE, D), row),
                  pl.BlockSpec((1, 1, 6 * D), lambda i: (_mod_row(i), 0, 0)),
                  pl.BlockSpec((1, 1, D), lambda i: (2 * l, 0, 0)),
                  pl.BlockSpec((1, D, IN_WIDTH), lambda i: (l, 0, 0)),
                  pl.BlockSpec((1, 2, LANES), lambda i: (l, 0, 0)),
                  pl.BlockSpec((TOK_TILE, LANES), rope_blk),
                  pl.BlockSpec((TOK_TILE, LANES), rope_blk),
                  pl.BlockSpec((TOK_TILE, LANES), rope_blk),
                  pl.BlockSpec((TOK_TILE, LANES), rope_blk)],
        out_specs=out_specs,
        compiler_params=_cparams(("arbitrary",)),
        name=f"pre_{l}",
    )(x, mod, norm_g, w_in16, qkn, *tabs)


def _lane_half(shape, half):
    lane = lax.broadcasted_iota(jnp.int32, shape, 1)
    return (lane // HD) == half


def _keep_lanes(x, lo, width):
    lane = lax.broadcasted_iota(jnp.int32, x.shape, 1)
    return jnp.where((lane >= lo) & (lane < lo + width), x, jnp.zeros_like(x))


def _attend(q, kvs, scale, bias=None):
    ss = []
    for i, (k, _) in enumerate(kvs):
        s = lax.dot_general(q, k, _NT, preferred_element_type=F32) * scale
        if i == 0 and bias is not None:
            s = s + bias
        ss.append(s)
    m = jnp.max(ss[0], axis=-1, keepdims=True)
    for s in ss[1:]:
        m = jnp.maximum(m, jnp.max(s, axis=-1, keepdims=True))
    o, den = None, None
    for s, (_, v) in zip(ss, kvs):
        p = jnp.exp(s - m)
        ps = jnp.sum(p, axis=-1, keepdims=True)
        po = jnp.dot(p.astype(BF16), v, preferred_element_type=F32)
        o = po if o is None else o + po
        den = ps if den is None else den + ps
    return o / den


def _diff_lambda(dl_ref, lam_init):
    dl = dl_ref[0]
    a = jnp.sum(dl[0:1] * dl[1:2], axis=-1, keepdims=True)
    b = jnp.sum(dl[2:3] * dl[3:4], axis=-1, keepdims=True)
    return jnp.exp(a) - jnp.exp(b) + lam_init


def _mixer_b(bq, kvs_of_slab, lam, sub, lam_init):
    scale = DQK_B ** -0.5
    slabs = []
    for s in range(W_BV // LANES):
        halves = []
        for half in range(2):
            h = 2 * s + half
            maps = []
            for c in range(2):
                hc = 2 * h + c
                q = _keep_lanes(bq[:, (hc // 4) * LANES:(hc // 4 + 1) * LANES], (hc % 4) * DQK_B, DQK_B)
                maps.append(_attend(q, kvs_of_slab(hc // 4, s), scale))
            halves.append(maps[0] - lam * maps[1])
        od = jnp.where(_lane_half(halves[0].shape, 0), halves[0], halves[1])
        od = od * lax.rsqrt(_group_mean_sq(od, DV_B) + EPS) * sub * (1.0 - lam_init)
        slabs.append(od)
    return jnp.concatenate(slabs, axis=1)


def _mixer_c(cq, kvs):
    slabs = []
    for s in range(W_CQ // LANES):
        qs = cq[:, s * LANES:(s + 1) * LANES]
        o0 = _attend(_keep_lanes(qs, 0, HD), kvs, HD ** -0.5)
        o1 = _attend(_keep_lanes(qs, HD, HD), kvs, HD ** -0.5)
        slabs.append(jnp.where(_lane_half(o0.shape, 0), o0, o1))
    return jnp.concatenate(slabs, axis=1)


def _ctx_attn_kernel(aq_ref, ak_ref, av_ref, bq_ref, bk_ref, bv_ref, cq_ref, ck_ref, cv_ref,
                     dl_ref, sub_ref, o_ref, *, lam_init):
    lam = _diff_lambda(dl_ref, lam_init)
    aq, ak, av = aq_ref[...], ak_ref[...], av_ref[...]
    slabs = []
    for s in range(W_A // LANES):
        sl = slice(s * LANES, (s + 1) * LANES)
        kvs = [(ak[:, sl], av[:, sl])]
        o0 = _attend(_keep_lanes(aq[:, sl], 0, HD), kvs, HD ** -0.5)
        o1 = _attend(_keep_lanes(aq[:, sl], HD, HD), kvs, HD ** -0.5)
        slabs.append(jnp.where(_lane_half(o0.shape, 0), o0, o1))
    oa = jnp.concatenate(slabs, axis=1)

    bk, bv = bk_ref[...], bv_ref[...]
    ob = _mixer_b(bq_ref[...],
                  lambda ks, vs: [(bk[:, ks * LANES:(ks + 1) * LANES], bv[:, vs * LANES:(vs + 1) * LANES])],
                  lam, sub_ref[0], lam_init)
    oc = _mixer_c(cq_ref[...], [(ck_ref[...], cv_ref[...])])
    o_ref[...] = jnp.concatenate([oa, ob, oc], axis=1).astype(BF16)


def _ctx_attention(l, qkv, diff_lambda, sub_tiled):
    widths = (W_A, W_A, W_A, W_BQK, W_BQK, W_BV, W_CQ, W_CKV, W_CKV)
    lam_init = 0.8 - 0.6 * math.exp(-0.3 * l)
    return pl.pallas_call(
        functools.partial(_ctx_attn_kernel, lam_init=lam_init),
        out_shape=jax.ShapeDtypeStruct((N_CTX, D), BF16),
        grid=(BATCH,),
        in_specs=[pl.BlockSpec((SEQ, w), lambda b: (b, 0)) for w in widths] + [
            pl.BlockSpec((1, 4, DQK_B), lambda b: (l, 0, 0)),
            pl.BlockSpec((1, 1, LANES), lambda b: (l, 0, 0))],
        out_specs=pl.BlockSpec((SEQ, D), lambda b: (b, 0)),
        compiler_params=_cparams(("arbitrary",)),
        name=f"ctx_attn_{l}",
    )(*qkv, diff_lambda, sub_tiled)


def _lat_attn_kernel(aq_ref, ak_ref, av_ref, cak_ref, cav_ref, bias0_ref, bias1_ref,
                     bq_ref, bk_ref, bv_ref, cbk_ref, cbv_ref,
                     cq_ref, ck_ref, cv_ref, cck_ref, ccv_ref,
                     dl_ref, sub_ref, o_ref, *, lam_init):
    qb = pl.program_id(1)
    lam = _diff_lambda(dl_ref, lam_init)

    cak = cak_ref[0, 0].astype(BF16)
    cav = cav_ref[0, 0].astype(BF16)
    rows = []
    for pidx, bias_ref in enumerate((bias0_ref, bias1_ref)):
        j = 2 * qb + pidx
        start = pl.multiple_of(jnp.clip(j - 2, 0, ROWS // 2 - 5) * NA_PAIR, NA_PAIR)
        kwin = ak_ref[pl.ds(start, NA_KEYS), :]
        vwin = av_ref[pl.ds(start, NA_KEYS), :]
        q = aq_ref[pidx * NA_PAIR:(pidx + 1) * NA_PAIR, :]
        slabs = []
        for s in range(W_A // LANES):
            sl = slice(s * LANES, (s + 1) * LANES)
            kvs = [(kwin[:, sl], vwin[:, sl]), (cak[:, sl], cav[:, sl])]
            o0 = _attend(_keep_lanes(q[:, sl], 0, HD), kvs, HD ** -0.5, bias_ref[0, 2 * s])
            o1 = _attend(_keep_lanes(q[:, sl], HD, HD), kvs, HD ** -0.5, bias_ref[0, 2 * s + 1])
            slabs.append(jnp.where(_lane_half(o0.shape, 0), o0, o1))
        rows.append(jnp.concatenate(slabs, axis=1))
    oa = jnp.concatenate(rows, axis=0)

    bk, bv = bk_ref[...], bv_ref[...]
    cbk = cbk_ref[0, 0].astype(BF16)
    cbv = cbv_ref[0, 0].astype(BF16)
    ob = _mixer_b(bq_ref[...],
                  lambda ks, vs: [(bk[:, ks * LANES:(ks + 1) * LANES], bv[:, vs * LANES:(vs + 1) * LANES]),
                                  (cbk[:, ks * LANES:(ks + 1) * LANES], cbv[:, vs * LANES:(vs + 1) * LANES])],
                  lam, sub_ref[0], lam_init)
    oc = _mixer_c(cq_ref[...], [(ck_ref[...], cv_ref[...]),
                                (cck_ref[0, 0].astype(BF16), ccv_ref[0, 0].astype(BF16))])
    o_ref[...] = jnp.concatenate([oa, ob, oc], axis=1).astype(BF16)


def _lat_attention(l, qkv, caches, bias0, bias1, diff_lambda, sub_tiled):
    aq, ak, av, bq, bk, bv, cq, ck, cv = qkv
    cak, cav, cbk, cbv, cck, ccv = caches
    lam_init = 0.8 - 0.6 * math.exp(-0.3 * l)
    nqb = DEC_SEQ // TOK_TILE
    lat0 = N_CTX // TOK_TILE
    qrow = lambda b, q: (lat0 + b * nqb + q, 0)
    krow = lambda b, q: (N_CTX // DEC_SEQ + b, 0)
    cache = lambda b, q: (b, l, 0, 0)
    nb = nqb - 1
    return pl.pallas_call(
        functools.partial(_lat_attn_kernel, lam_init=lam_init),
        out_shape=jax.ShapeDtypeStruct((N_LAT, D), BF16),
        grid=(DEC_BATCH, nqb),
        in_specs=[pl.BlockSpec((TOK_TILE, W_A), qrow),
                  pl.BlockSpec((DEC_SEQ, W_A), krow),
                  pl.BlockSpec((DEC_SEQ, W_A), krow),
                  pl.BlockSpec((1, 1, PAST, W_A), cache),
                  pl.BlockSpec((1, 1, PAST, W_A), cache),
                  pl.BlockSpec((1, H_A, NA_PAIR, NA_KEYS),
                               lambda b, q: (l * 3 + jnp.where(q == 0, 0, jnp.where(q == nb, 2, 1)), 0, 0, 0)),
                  pl.BlockSpec((1, H_A, NA_PAIR, NA_KEYS),
                               lambda b, q: (l * 3 + jnp.where(q == 0, 0, jnp.where(q == nb, 2, 1)), 0, 0, 0)),
                  pl.BlockSpec((TOK_TILE, W_BQK), qrow),
                  pl.BlockSpec((DEC_SEQ, W_BQK), krow),
                  pl.BlockSpec((DEC_SEQ, W_BV), krow),
                  pl.BlockSpec((1, 1, PAST, W_BQK), cache),
                  pl.BlockSpec((1, 1, PAST, W_BV), cache),
                  pl.BlockSpec((TOK_TILE, W_CQ), qrow),
                  pl.BlockSpec((DEC_SEQ, W_CKV), krow),
                  pl.BlockSpec((DEC_SEQ, W_CKV), krow),
                  pl.BlockSpec((1, 1, PAST, W_CKV), cache),
                  pl.BlockSpec((1, 1, PAST, W_CKV), cache),
                  pl.BlockSpec((1, 4, DQK_B), lambda b, q: (l, 0, 0)),
                  pl.BlockSpec((1, 1, LANES), lambda b, q: (l, 0, 0))],
        out_specs=pl.BlockSpec((TOK_TILE, D), lambda b, q: (b * nqb + q, 0)),
        compiler_params=_cparams(("arbitrary", "arbitrary")),
        name=f"lat_attn_{l}",
    )(aq, ak, av, cak, cav, bias0, bias1, bq, bk, bv, cbk, cbv, cq, ck, cv, cck, ccv,
      diff_lambda, sub_tiled)


def _post_kernel(oc_ref, ol_ref, x_ref, mod_ref, g_ref, w_ref, x1_ref, h2t_ref):
    mod = mod_ref[0]
    gate1, shift2, scale2 = mod[:, 2 * D:3 * D], mod[:, 3 * D:4 * D], mod[:, 4 * D:5 * D]
    o = jnp.where(pl.program_id(0) < CTX_TILES, oc_ref[...], ol_ref[...])
    x1 = x_ref[...] + gate1 * jnp.dot(o, w_ref[0], preferred_element_type=F32)
    x1_ref[...] = x1
    h2 = _rms(x1, g_ref[0]) * (1.0 + scale2) + shift2
    h2t_ref[...] = h2.T.astype(BF16)


def _post(l, o_ctx, o_lat, x, mod, norm_g, w_out16):
    row = lambda i: (i, 0)
    return pl.pallas_call(
        _post_kernel,
        out_shape=(jax.ShapeDtypeStruct((N_TOK, D), F32),
                   jax.ShapeDtypeStruct((D, N_TOK), BF16)),
        grid=(N_TILES,),
        in_specs=[pl.BlockSpec((TOK_TILE, D), lambda i: (jnp.minimum(i, CTX_TILES - 1), 0)),
                  pl.BlockSpec((TOK_TILE, D), lambda i: (jnp.maximum(i - CTX_TILES, 0), 0)),
                  pl.BlockSpec((TOK_TILE, D), row),
                  pl.BlockSpec((1, 1, 6 * D), lambda i: (_mod_row(i), 0, 0)),
                  pl.BlockSpec((1, 1, D), lambda i: (2 * l + 1, 0, 0)),
                  pl.BlockSpec((1, D, D), lambda i: (l, 0, 0))],
        out_specs=(pl.BlockSpec((TOK_TILE, D), row),
                   pl.BlockSpec((D, TOK_TILE), lambda i: (0, i))),
        compiler_params=_cparams(("arbitrary",)),
        name=f"post_{l}",
    )(o_ctx, o_lat, x, mod, norm_g, w_out16)


def _top16(x):
    kio = lax.broadcasted_iota(jnp.int32, x.shape, 0).astype(F32)
    rank = jnp.full(x.shape, 99.0, F32)
    vals = []
    for r in range(PEER_TOPK):
        m = jnp.max(x, axis=0, keepdims=True)
        kmin = jnp.min(jnp.where(x == m, kio, float(N_KEYS)), axis=0, keepdims=True)
        sel = kio == kmin
        rank = jnp.where(sel, float(r), rank)
        x = jnp.where(sel, -jnp.inf, x)
        vals.append(m)
    return rank, jnp.concatenate(vals, axis=0)


def _merge16(pairs):
    io = lax.broadcasted_iota(jnp.int32, pairs[0][0].shape, 0).astype(F32)
    cnt = [jnp.zeros(s1r.shape, F32) for s1r, _ in pairs]
    head = [s1r + s2r[0:1] for s1r, s2r in pairs]
    top = [hd[0:1] for hd in head]
    z = [jnp.zeros_like(t) for t in top]
    for _ in range(PEER_TOPK):
        for i, (s1r, s2r) in enumerate(pairs):
            m = jnp.max(head[i], axis=0, keepdims=True)
            z[i] = z[i] + jnp.exp(m - top[i])
            row = jnp.min(jnp.where(head[i] == m, io, 99.0), axis=0, keepdims=True)
            hit = io == row
            cnt[i] = cnt[i] + jnp.where(hit, 1.0, 0.0)
            nxt_idx = jnp.max(jnp.where(hit, cnt[i], 0.0), axis=0, keepdims=True)
            nxt = jnp.max(jnp.where(io == nxt_idx, s2r, -jnp.inf), axis=0, keepdims=True)
            head[i] = jnp.where(hit, s1r + nxt, head[i])
    return list(zip(cnt, z))


def _topk_kernel(h2t_ref, wqt_ref, keys_ref, rank2_ref, e2_ref, lk_ref, cw_ref, st_ref, sr_ref):
    qt = jnp.dot(wqt_ref[0], h2t_ref[...], preferred_element_type=F32).astype(BF16)
    for hc in range(2 * PEER_HEADS):
        st_ref[hc] = jnp.dot(keys_ref[0, hc], qt[hc * N_KEYS:(hc + 1) * N_KEYS, :],
                             preferred_element_type=F32)

    def head_body(h, carry):
        s1 = st_ref[2 * h]
        s2 = st_ref[2 * h + 1]
        rank1, s1r = _top16(s1)
        rank2, s2r = _top16(s2)
        sr_ref[2 * h] = s1r
        sr_ref[2 * h + 1] = s2r
        rank2_ref[h] = rank2.astype(BF16)
        e2_ref[h] = jnp.exp(s2 - s2r[0:1]).astype(BF16)
        lk_ref[h] = rank1
        cw_ref[h] = jnp.exp(s1 - s1r[0:1])
        return carry

    lax.fori_loop(0, PEER_HEADS, head_body, 0, unroll=2)

    merged = _merge16([(sr_ref[2 * h], sr_ref[2 * h + 1]) for h in range(PEER_HEADS)])
    for h, (cnt, z) in enumerate(merged):
        rank1 = lk_ref[h]
        lk = jnp.zeros_like(rank1)
        for r in range(PEER_TOPK):
            lk = jnp.where(rank1 == float(r), cnt[r:r + 1], lk)
        lk_ref[h] = lk
        cw_ref[h] = cw_ref[h] / z


def _topk(l, h2t, wqt16, keys16):
    blk = pl.BlockSpec((PEER_HEADS, N_KEYS, TOPK_CHUNK), lambda i: (0, 0, i))
    shp = jax.ShapeDtypeStruct((PEER_HEADS, N_KEYS, N_TOK), F32)
    shp16 = jax.ShapeDtypeStruct((PEER_HEADS, N_KEYS, N_TOK), BF16)
    return pl.pallas_call(
        _topk_kernel,
        out_shape=(shp16, shp16, shp, shp),
        grid=(N_TOK // TOPK_CHUNK,),
        in_specs=[pl.BlockSpec((D, TOPK_CHUNK), lambda i: (0, i)),
                  pl.BlockSpec((1, 2 * PEER_HEADS * N_KEYS, D), lambda i: (l, 0, 0)),
                  pl.BlockSpec((1, 2 * PEER_HEADS, N_KEYS, N_KEYS), lambda i: (l, 0, 0, 0))],
        out_specs=(blk, blk, blk, blk),
        scratch_shapes=[pltpu.VMEM((2 * PEER_HEADS, N_KEYS, TOPK_CHUNK), F32),
                        pltpu.VMEM((2 * PEER_HEADS, PEER_TOPK, TOPK_CHUNK), F32)],
        compiler_params=_cparams(("arbitrary",)),
        name=f"topk_{l}",
    )(h2t, wqt16, keys16)


def _gelu(x):
    c = math.sqrt(2.0 / math.pi)
    return 0.5 * x * (1.0 + jnp.tanh(c * (x + 0.044715 * (x * x * x))))


def _expert_kernel(h2t_ref, u_ref, vt_ref, rank2_ref, e2_ref, lk_ref, cw_ref, x1_ref, mod_ref,
                   o_ref, acc_ref, at0_ref, at1_ref):
    j = pl.program_id(1)
    nblk = pl.num_programs(1) - 1

    @pl.when(j == 0)
    def _():
        acc_ref[...] = jnp.zeros_like(acc_ref)
        at1_ref[...] = jnp.zeros_like(at1_ref)

    def step(at_w, at_r):
        at_w[...] = jnp.dot(u_ref[0], h2t_ref[...], preferred_element_type=F32)
        part = None
        per = EXP_CHUNK // N_KEYS
        for c in range(EXP_BLK // EXP_CHUNK):
            ags = []
            for a in range(c * per, (c + 1) * per):
                g = None
                for h in range(PEER_HEADS):
                    lk = lk_ref[h, a:a + 1, :].astype(BF16)
                    cw = cw_ref[h, a:a + 1, :].astype(BF16)
                    contrib = jnp.where(rank2_ref[h] < lk, e2_ref[h] * cw, jnp.zeros((), BF16))
                    g = contrib if g is None else g + contrib
                ags.append(_gelu(at_r[a * N_KEYS:(a + 1) * N_KEYS, :]).astype(BF16) * g)
            d = jnp.dot(vt_ref[0, :, c * EXP_CHUNK:(c + 1) * EXP_CHUNK], jnp.concatenate(ags, axis=0),
                        preferred_element_type=F32)
            part = d if part is None else part + d
        acc_ref[...] += part

    @pl.when(j % 2 == 0)
    def _():
        step(at0_ref, at1_ref)

    @pl.when(j % 2 == 1)
    def _():
        step(at1_ref, at0_ref)

    @pl.when(j == nblk)
    def _():
        gate2 = mod_ref[0][:, 5 * D:6 * D]
        o_ref[...] = x1_ref[...] + gate2 * acc_ref[...].T


def _experts(l, h2t, u16, vt16, rank2, e2, lk, cw, x1, mod):
    nblk = N_EXPERTS // EXP_BLK
    tok = pl.BlockSpec((PEER_HEADS, N_KEYS, EXP_TOK), lambda i, j: (0, 0, i))
    sel = pl.BlockSpec((PEER_HEADS, EXP_BLK // N_KEYS, EXP_TOK), lambda i, j: (0, jnp.maximum(j - 1, 0), i))
    return pl.pallas_call(
        _expert_kernel,
        out_shape=jax.ShapeDtypeStruct((N_TOK, D), F32),
        grid=(N_TOK // EXP_TOK, nblk + 1),
        in_specs=[pl.BlockSpec((D, EXP_TOK), lambda i, j: (0, i)),
                  pl.BlockSpec((1, EXP_BLK, D), lambda i, j: (l, jnp.minimum(j, nblk - 1), 0)),
                  pl.BlockSpec((1, D, EXP_BLK), lambda i, j: (l, 0, jnp.maximum(j - 1, 0))),
                  tok, tok, sel, sel,
                  pl.BlockSpec((EXP_TOK, D), lambda i, j: (i, 0)),
                  pl.BlockSpec((1, 1, 6 * D), lambda i, j: (_mod_row(i * (EXP_TOK // TOK_TILE)), 0, 0))],
        out_specs=pl.BlockSpec((EXP_TOK, D), lambda i, j: (i, 0)),
        scratch_shapes=[pltpu.VMEM((D, EXP_TOK), F32),
                        pltpu.VMEM((EXP_BLK, EXP_TOK), F32),
                        pltpu.VMEM((EXP_BLK, EXP_TOK), F32)],
        compiler_params=_cparams(("arbitrary", "arbitrary")),
        name=f"experts_{l}",
    )(h2t, u16, vt16, rank2, e2, lk, cw, x1, mod)


def _final_kernel(x_ref, g_ref, o_ref):
    o_ref[...] = _rms(x_ref[...], g_ref[...])


def _final_norm(x, g):
    row = lambda i: (i, 0)
    return pl.pallas_call(
        _final_kernel,
        out_shape=jax.ShapeDtypeStruct((N_TOK, D), F32),
        grid=(N_TILES,),
        in_specs=[pl.BlockSpec((TOK_TILE, D), row), pl.BlockSpec((1, D), lambda i: (0, 0))],
        out_specs=pl.BlockSpec((TOK_TILE, D), row),
        compiler_params=_cparams(("arbitrary",)),
        name="final_norm",
    )(x, g.reshape(1, D))


def _rope_tables():
    t = jnp.arange(DEC_SEQ)
    rowp = (t // GRID_W).astype(F32)[:, None]
    colp = (t % GRID_W).astype(F32)[:, None]
    lane = np.arange(LANES)

    def tab(d):
        q = d // 4
        dd = lane % d
        inv = ROPE_THETA ** (-jnp.arange(q, dtype=F32) / q)
        ang = jnp.where((dd < d // 2)[None, :], rowp, colp) * inv[dd % q][None, :]
        sign = np.where((dd % (d // 2)) < q, -1.0, 1.0).astype(np.float32)[None, :]
        cos = jnp.concatenate([jnp.ones((TOK_TILE, LANES), F32), jnp.cos(ang)], axis=0)
        sin = jnp.concatenate([jnp.zeros((TOK_TILE, LANES), F32), jnp.sin(ang) * sign], axis=0)
        return cos, sin

    cb, sb = tab(DQK_B)
    cc, sc = tab(HD)
    return cb, sb, cc, sc


def _na_bias_index():
    def one(j):
        m0 = min(max(j - 2, 0), ROWS // 2 - 5)
        r = 2 * j + np.arange(2)
        rs = np.clip(r - NA_ROWS // 2, 0, ROWS - NA_ROWS)
        krow = 2 * m0 + np.arange(NA_KEYS // GRID_W)
        row_ok = (krow[None, :] >= rs[:, None]) & (krow[None, :] < rs[:, None] + NA_ROWS)
        dr = np.clip(krow[None, :] - r[:, None] + NA_ROWS - 1, 0, 2 * NA_ROWS - 2)
        qc = np.arange(GRID_W)
        cs = np.clip(qc - NA_COLS // 2, 0, GRID_W - NA_COLS)
        col_ok = (qc[None, :] >= cs[:, None]) & (qc[None, :] < cs[:, None] + NA_COLS)
        ok = row_ok[:, None, :, None] & col_ok[None, :, None, :]
        return dr, ok.reshape(NA_PAIR, NA_KEYS)

    mid = ROWS // 4
    first = [one(j) for j in (0, mid, ROWS // 2 - 2)]
    second = [one(j) for j in (1, mid, ROWS // 2 - 1)]
    stack = lambda xs, k: np.stack([x[k] for x in xs])
    return (stack(first, 0), stack(first, 1)), (stack(second, 0), stack(second, 1))


def _na_bias(na_rpb, dr, ok):
    qc = np.arange(GRID_W)
    dc = np.clip(qc[None, :] - qc[:, None], -(NA_COLS - 1), NA_COLS - 1) + NA_COLS - 1
    onehot = (dc[None] == np.arange(2 * NA_COLS - 1)[:, None, None]).astype(np.float32)
    rows = jnp.einsum('lhrd,dn->lhrn', na_rpb, onehot.reshape(2 * NA_COLS - 1, GRID_W * GRID_W),
                      precision=lax.Precision.HIGHEST)
    nkr = NA_KEYS // GRID_W
    t = rows[:, :, dr].reshape(DEPTH, H_A, 3, 2, nkr, GRID_W, GRID_W)
    t = t.transpose(0, 2, 1, 3, 5, 4, 6).reshape(DEPTH, 3, H_A, NA_PAIR, NA_KEYS)
    return jnp.where(ok[None, :, None], t, NEG).reshape(DEPTH * 3, H_A, NA_PAIR, NA_KEYS)


def kernel(x_prompt, x_sample, cache_a_k, cache_a_v, cache_b_k, cache_b_v, cache_c_k, cache_c_v,
           c, c_ctx, w_mod, b_mod, norm_g, w_in, w_out, na_rpb, diff_lambda, diff_subln,
           gqa_qk_norm, peer_wq, peer_keys, peer_u, peer_v, final_g):
    cq0 = 3 * W_A + 2 * W_BQK + W_BV
    g = H_C // KV_C
    w_cq = w_in[:, :, cq0:cq0 + W_CQ].reshape(DEPTH, D, KV_C, g, HD).transpose(0, 1, 3, 2, 4)
    w_in16 = jnp.concatenate([w_in[:, :, :cq0], w_cq.reshape(DEPTH, D, W_CQ), w_in[:, :, cq0 + W_CQ:]],
                             axis=2).astype(BF16)
    oc0 = W_A + W_BV
    w_oc = w_out[:, oc0:].reshape(DEPTH, KV_C, g, HD, D).transpose(0, 2, 1, 3, 4)
    w_out16 = jnp.concatenate([w_out[:, :oc0], w_oc.reshape(DEPTH, W_CQ, D)], axis=1).astype(BF16)
    wqt16 = peer_wq.transpose(0, 2, 1).astype(BF16)
    keys16 = peer_keys.reshape(DEPTH, 2 * PEER_HEADS, N_KEYS, N_KEYS).astype(BF16)
    u16, vt16 = _prep_experts(peer_u, peer_v)

    cond = jnp.concatenate([c_ctx[None, :], c, jnp.zeros((3, D), F32)], axis=0)
    mod_all = _modulation(cond, w_mod, b_mod)[:, :1 + DEC_BATCH].reshape(DEPTH, 1 + DEC_BATCH, 1, 6 * D)
    norm_g3 = norm_g.reshape(2 * DEPTH, 1, D)
    qkn = jnp.tile(gqa_qk_norm, (1, 1, LANES // HD))
    sub_tiled = jnp.tile(diff_subln, (1, LANES // DV_B)).reshape(DEPTH, 1, LANES)
    tabs = _rope_tables()
    (dr0, ok0), (dr1, ok1) = _na_bias_index()
    bias0 = _na_bias(na_rpb, dr0, ok0)
    bias1 = _na_bias(na_rpb, dr1, ok1)
    caches = (cache_a_k.reshape(DEC_BATCH, DEPTH, PAST, W_A), cache_a_v.reshape(DEC_BATCH, DEPTH, PAST, W_A),
              cache_b_k.reshape(DEC_BATCH, DEPTH, PAST, W_BQK), cache_b_v.reshape(DEC_BATCH, DEPTH, PAST, W_BV),
              cache_c_k.reshape(DEC_BATCH, DEPTH, PAST, W_CKV), cache_c_v.reshape(DEC_BATCH, DEPTH, PAST, W_CKV))

    x = jnp.concatenate([x_prompt.reshape(N_CTX, D), x_sample.reshape(N_LAT, D)], axis=0)
    new_kv = []
    for l in range(DEPTH):
        mod = mod_all[l]
        *qkv, kv = _pre(l, x, mod, norm_g3, w_in16, qkn, tabs)
        new_kv.append(kv[:N_CTX].reshape(BATCH, SEQ, KV_WIDTH))
        o_ctx = _ctx_attention(l, qkv, diff_lambda, sub_tiled)
        o_lat = _lat_attention(l, qkv, caches, bias0, bias1, diff_lambda, sub_tiled)
        x1, h2t = _post(l, o_ctx, o_lat, x, mod, norm_g3, w_out16)
        rank2, e2, lk, cw = _topk(l, h2t, wqt16, keys16)
        x = _experts(l, h2t, u16, vt16, rank2, e2, lk, cw, x1, mod)
    y = _final_norm(x, final_g)

    kv = jnp.stack(new_kv, axis=1)
    o = 0
    parts = []
    for w, shp in ((W_A, (H_A, HD)), (W_A, (H_A, HD)), (W_BQK, (H_B, 2, DQK_B)),
                   (W_BV, (H_B, DV_B)), (W_CKV, (KV_C, HD)), (W_CKV, (KV_C, HD))):
        parts.append(kv[..., o:o + w].reshape(BATCH, DEPTH, SEQ, *shp))
        o += w
    return (y[:N_CTX].reshape(BATCH, SEQ, D), y[N_CTX:].reshape(DEC_BATCH, DEC_SEQ, D), *parts)
```
